```python
import jax, jax.numpy as jnp
from jax import lax
import numpy as np

D_MODEL = 1024
BATCH = 4
SEQ = 4096
DEPTH = 2
DEC_BATCH = 128
DEC_SEQ = 4
PAST_LEN = 8192
PAGE_SIZE = 128

A_WIDTH = D_MODEL // 4
CONV_W = 3
HEAD_DIM = 64
N_Q = (D_MODEL // 2) // HEAD_DIM
N_KV = 2
Q_PER_KV = N_Q // N_KV
B_WIDTH = N_Q * HEAD_DIM
KV_WIDTH = N_KV * HEAD_DIM
WINDOW = 128
CHUNK = 128
C_GROUPS = 4
C_WIDTH = D_MODEL // 4
C_GDIM = C_WIDTH // C_GROUPS
N_BRANCH = 3
D_FF = 2816
FFN_CONV_W = 3
EPS = 1e-6
NEG = -1e30

SPLITS = (A_WIDTH, A_WIDTH, A_WIDTH, B_WIDTH, KV_WIDTH, KV_WIDTH, C_WIDTH, C_WIDTH, N_BRANCH * D_MODEL)
PROJ_WIDTH = sum(SPLITS)
SPLIT_IDX = tuple(int(s) for s in np.cumsum(SPLITS)[:-1])

kernel_name = 'hybrid_gated_conv_swa_chunkmlp_step'


def rmsnorm(x, g):
    xf = x.astype(jnp.float32)
    y = xf * lax.rsqrt(jnp.mean(xf * xf, axis=-1, keepdims=True) + EPS)
    return (y * g.astype(jnp.float32)).astype(x.dtype)


def causal_dwconv(z, prev, w, b=None):
    width = w.shape[0]
    t = z.shape[1]
    full = jnp.concatenate([prev.astype(z.dtype), z], axis=1)
    y = sum(w[i] * full[:, i:i + t] for i in range(width))
    if b is not None:
        y = y + b
    return y, full[:, full.shape[1] - (width - 1):]


def sink_softmax(s, sinks):
    sk = sinks.astype(jnp.float32)[..., None, None]
    m = jnp.maximum(jnp.max(s, axis=-1, keepdims=True), sk)
    p = jnp.exp(s - m)
    return p / (jnp.sum(p, axis=-1, keepdims=True) + jnp.exp(sk - m))


def window_attn_prompt(q, k, v, sinks):
    bsz, s_len = q.shape[0], q.shape[1]
    nb = s_len // WINDOW
    qb = q.reshape(bsz, nb, WINDOW, N_KV, Q_PER_KV, HEAD_DIM)
    kb = k.reshape(bsz, nb, WINDOW, N_KV, HEAD_DIM)
    vb = v.reshape(bsz, nb, WINDOW, N_KV, HEAD_DIM)
    kk = jnp.concatenate([jnp.concatenate([jnp.zeros_like(kb[:, :1]), kb[:, :-1]], axis=1), kb], axis=2)
    vv = jnp.concatenate([jnp.concatenate([jnp.zeros_like(vb[:, :1]), vb[:, :-1]], axis=1), vb], axis=2)
    s = jnp.einsum('bnqhgd,bnkhd->bnhgqk', qb, kk, preferred_element_type=jnp.float32) * (HEAD_DIM ** -0.5)
    i = jnp.arange(WINDOW)[:, None]
    j = jnp.arange(2 * WINDOW)[None, :]
    diff = i + WINDOW - j
    band = (diff >= 0) & (diff < WINDOW)
    has_prev = jnp.arange(nb)[:, None, None] > 0
    mask = band[None] & (has_prev | (j >= WINDOW)[None])
    p = sink_softmax(jnp.where(mask[None, :, None, None], s, NEG), sinks)
    o = jnp.einsum('bnhgqk,bnkhd->bnqhgd', p.astype(vv.dtype), vv)
    rows = min(WINDOW, s_len)
    return o.reshape(bsz, s_len, B_WIDTH), k[:, s_len - rows:], v[:, s_len - rows:]


def window_attn_sample(q, k, v, ck, cv, sinks):
    bsz, t = q.shape[0], q.shape[1]
    r = ck.shape[1]
    kk = jnp.concatenate([ck.astype(k.dtype), k], axis=1)
    vv = jnp.concatenate([cv.astype(v.dtype), v], axis=1)
    qg = q.reshape(bsz, t, N_KV, Q_PER_KV, HEAD_DIM)
    s = jnp.einsum('bqhgd,bkhd->bhgqk', qg, kk, preferred_element_type=jnp.float32) * (HEAD_DIM ** -0.5)
    diff = jnp.arange(t)[:, None] + r - jnp.arange(r + t)[None, :]
    mask = (diff >= 0) & (diff < WINDOW)
    p = sink_softmax(jnp.where(mask, s, NEG), sinks)
    o = jnp.einsum('bhgqk,bkhd->bqhgd', p.astype(vv.dtype), vv)
    return o.reshape(bsz, t, B_WIDTH), kk[:, t:], vv[:, t:]


def chunk_spatial_gate(u, v, w_s, b_s):
    L = v.shape[2]
    tri = jnp.tril(jnp.ones((L, L), dtype=bool))
    w = jnp.where(tri, w_s[:, :L, :L], jnp.zeros((), w_s.dtype))
    mixed = jnp.einsum('gts,bnsgc->bntgc', w, v) + b_s[:, :L].T[None, None, :, :, None]
    return u * mixed


def hybrid_layer(x, conv_a_prev, win_k, win_v, ffn_prev,
                 w_in, conv_a_w, sinks, w_s, b_s, g_v, w_ba, w_bb, w_bc, w_o,
                 g_pre, g_post, g_pre_f, g_post_f, w_up, conv_f_w, conv_f_b, w_down):
    bsz, t, _ = x.shape
    h = rmsnorm(x, g_pre)
    p = h @ w_in
    a_in, a_b, a_c, q, k, v, c_u, c_v, gates = jnp.split(p, SPLIT_IDX, axis=-1)
    za, conv_a_new = causal_dwconv(a_c * a_in, conv_a_prev, conv_a_w)
    ya = a_b * za
    q = q.reshape(bsz, t, N_Q, HEAD_DIM)
    k = k.reshape(bsz, t, N_KV, HEAD_DIM)
    v = v.reshape(bsz, t, N_KV, HEAD_DIM)
    if win_k is None:
        yb, k_new, v_new = window_attn_prompt(q, k, v, sinks)
    else:
        yb, k_new, v_new = window_attn_sample(q, k, v, win_k, win_v, sinks)
    L = min(t, CHUNK)
    n = t // L
    vn = rmsnorm(c_v, g_v)
    yc = chunk_spatial_gate(c_u.reshape(bsz, n, L, C_GROUPS, C_GDIM),
                            vn.reshape(bsz, n, L, C_GROUPS, C_GDIM), w_s, b_s).reshape(bsz, t, C_WIDTH)
    g = jax.nn.sigmoid(gates.astype(jnp.float32)).astype(x.dtype).reshape(bsz, t, N_BRANCH, D_MODEL)
    merged = g[:, :, 0] * (ya @ w_ba) + g[:, :, 1] * (yb @ w_bb) + g[:, :, 2] * (yc @ w_bc)
    x = x + rmsnorm(merged @ w_o, g_post)
    up = rmsnorm(x, g_pre_f) @ w_up
    upc, ffn_new = causal_dwconv(up, ffn_prev, conv_f_w, conv_f_b)
    fa, fb = jnp.split(upc, 2, axis=-1)
    x = x + rmsnorm((jax.nn.silu(fa) * fb) @ w_down, g_post_f)
    return x, conv_a_new, k_new, v_new, ffn_new, vn


def setup_inputs(seed: int = 0) -> dict:
    key = jax.random.key(seed)
    ks = iter(jax.random.split(key, 32))
    f32 = jnp.float32
    nrm = lambda shape, scale: jax.random.normal(next(ks), shape, f32) * scale
    win_rows = min(WINDOW, PAST_LEN)
    return {
        'x_prompt': nrm((BATCH, SEQ, D_MODEL), 1.0),
        'x_sample': nrm((DEC_BATCH, DEC_SEQ, D_MODEL), 1.0),
        'state_conv_a': nrm((DEPTH, DEC_BATCH, CONV_W - 1, A_WIDTH), 1.0),
        'cache_win_k': nrm((DEPTH, DEC_BATCH, win_rows, N_KV, HEAD_DIM), 1.0),
        'cache_win_v': nrm((DEPTH, DEC_BATCH, win_rows, N_KV, HEAD_DIM), 1.0),
        'state_ffn_conv': nrm((DEPTH, DEC_BATCH, FFN_CONV_W - 1, 2 * D_FF), 1.0),
        'w_in': nrm((DEPTH, D_MODEL, PROJ_WIDTH), D_MODEL ** -0.5),
        'conv_a_w': nrm((DEPTH, CONV_W, A_WIDTH), CONV_W ** -0.5),
        'attn_sinks': nrm((DEPTH, N_KV, Q_PER_KV), 1.0),
        'spatial_w': nrm((DEPTH, C_GROUPS, CHUNK, CHUNK), CHUNK ** -0.5),
        'spatial_b': 1.0 + nrm((DEPTH, C_GROUPS, CHUNK), 0.1),
        'g_v': 1.0 + nrm((DEPTH, C_WIDTH), 0.05),
        'w_branch_a': nrm((DEPTH, A_WIDTH, D_MODEL), A_WIDTH ** -0.5),
        'w_branch_b': nrm((DEPTH, B_WIDTH, D_MODEL), B_WIDTH ** -0.5),
        'w_branch_c': nrm((DEPTH, C_WIDTH, D_MODEL), C_WIDTH ** -0.5),
        'w_out': nrm((DEPTH, D_MODEL, D_MODEL), D_MODEL ** -0.5),
        'g_pre_mix': 1.0 + nrm((DEPTH, D_MODEL), 0.05),
        'g_post_mix': 1.0 + nrm((DEPTH, D_MODEL), 0.05),
        'g_pre_ffn': 1.0 + nrm((DEPTH, D_MODEL), 0.05),
        'g_post_ffn': 1.0 + nrm((DEPTH, D_MODEL), 0.05),
        'w_up': nrm((DEPTH, D_MODEL, 2 * D_FF), D_MODEL ** -0.5),
        'conv_ffn_w': nrm((DEPTH, FFN_CONV_W, 2 * D_FF), FFN_CONV_W ** -0.5),
        'conv_ffn_b': nrm((DEPTH, 2 * D_FF), 0.01),
        'w_down': nrm((DEPTH, D_FF, D_MODEL), D_FF ** -0.5),
    }


def reference(x_prompt, x_sample, state_conv_a, cache_win_k, cache_win_v, state_ffn_conv,
              w_in, conv_a_w, attn_sinks, spatial_w, spatial_b, g_v, w_branch_a, w_branch_b,
              w_branch_c, w_out, g_pre_mix, g_post_mix, g_pre_ffn, g_post_ffn, w_up,
              conv_ffn_w, conv_ffn_b, w_down):
    yp, ys = x_prompt, x_sample
    bp = x_prompt.shape[0]
    ca_p, wk_p, wv_p, ff_p = [], [], [], []
    ca_s, wk_s, wv_s, ff_s, cv_s = [], [], [], [], []
    for l in range(DEPTH):
        wts = (w_in[l], conv_a_w[l], attn_sinks[l], spatial_w[l], spatial_b[l], g_v[l],
               w_branch_a[l], w_branch_b[l], w_branch_c[l], w_out[l], g_pre_mix[l], g_post_mix[l],
               g_pre_ffn[l], g_post_ffn[l], w_up[l], conv_ffn_w[l], conv_ffn_b[l], w_down[l])
        zeros_a = jnp.zeros((bp, CONV_W - 1, A_WIDTH), yp.dtype)
        zeros_f = jnp.zeros((bp, FFN_CONV_W - 1, 2 * D_FF), yp.dtype)
        yp, ca, k_new, v_new, ff, _ = hybrid_layer(yp, zeros_a, None, None, zeros_f, *wts)
        ca_p.append(ca); wk_p.append(k_new); wv_p.append(v_new); ff_p.append(ff)
        ys, ca, k_new, v_new, ff, vn = hybrid_layer(ys, state_conv_a[l], cache_win_k[l], cache_win_v[l],
                                                    state_ffn_conv[l], *wts)
        ca_s.append(ca); wk_s.append(k_new); wv_s.append(v_new); ff_s.append(ff); cv_s.append(vn)
    return (yp, ys,
            jnp.stack(ca_p), jnp.stack(wk_p), jnp.stack(wv_p), jnp.stack(ff_p),
            jnp.stack(ca_s), jnp.stack(wk_s), jnp.stack(wv_s), jnp.stack(ff_s), jnp.stack(cv_s))
```

```python
import functools

import jax
import jax.numpy as jnp
from jax import lax
from jax.experimental import pallas as pl
from jax.experimental.pallas import tpu as pltpu

F32 = jnp.float32
BF16 = jnp.bfloat16

HEAD_DIM = 64
N_KV = 2
Q_PER_KV = 4
WINDOW = 128
CHUNK = 128
C_GROUPS = 4
EPS = 1e-6
NEG = -1e30

VMEM_LIMIT_BYTES = 56 * 1024 * 1024
PROMPT_TILE = 256
SAMPLE_SEQ_GROUP = 8


def _rms(x, g):
    return x * lax.rsqrt(jnp.mean(x * x, axis=-1, keepdims=True) + EPS) * g


def _sigmoid(x):
    return 1.0 / (1.0 + jnp.exp(-x))


def _dot(a, b):
    return jnp.dot(a, b, preferred_element_type=F32)


def _dot_nt(a, b):
    return lax.dot_general(a, b, (((1,), (1,)), ((), ())), preferred_element_type=F32)


def _log2(n):
    assert n > 0 and n & (n - 1) == 0, f"{n} must be a power of two"
    return n.bit_length() - 1


def _div(a, n):
    return lax.shift_right_logical(a, _log2(n))


def _mod(a, n):
    return a & ((1 << _log2(n)) - 1)


def _softmax_pv(s, sk, v):
    m = jnp.maximum(jnp.max(s, axis=-1, keepdims=True), sk)
    p = jnp.exp(s - m)
    den = jnp.sum(p, axis=-1, keepdims=True) + jnp.exp(sk - m)
    return _dot(p.astype(BF16), v) * (1.0 / den)


def _sink_column(sinks_ref, kv_head, rows_per_head):
    n = Q_PER_KV * rows_per_head
    grp = _div(lax.broadcasted_iota(jnp.int32, (n, 1), 0), rows_per_head)
    col = jnp.full((n, 1), sinks_ref[kv_head * Q_PER_KV + Q_PER_KV - 1], F32)
    for g in range(Q_PER_KV - 2, -1, -1):
        col = jnp.where(grp == g, sinks_ref[kv_head * Q_PER_KV + g], col)
    return col


def _stack_heads(q, kv_head):
    base = kv_head * Q_PER_KV
    return jnp.concatenate(
        [q[:, (base + g) * HEAD_DIM:(base + g + 1) * HEAD_DIM] for g in range(Q_PER_KV)], axis=0)


def _gates(h, w_in_ref, gate_off, d_model, branch):
    lo = gate_off + branch * d_model
    return _sigmoid(_dot(h, w_in_ref[:, lo:lo + d_model]))


def _prompt_mixer_kernel(x_ref, w_in_ref, caw_ref, sinks_ref, ws_ref, bs_ref, gv_ref,
                         w_ba_ref, w_bb_ref, w_bc_ref, w_o_ref, gpre_ref, gpost_ref,
                         xo_ref, ca_ref, kn_ref, vn_ref,
                         zbuf, kbuf, vbuf, *, tm, d_model, a_w, c_w):
    t = pl.program_id(1)
    b_w = N_KV * Q_PER_KV * HEAD_DIM
    kv_w = N_KV * HEAD_DIM
    nb = tm // WINDOW
    off_q = 3 * a_w
    off_c = off_q + b_w + 2 * kv_w
    off_g = off_c + 2 * c_w

    @pl.when(t == 0)
    def _():
        zbuf[0:8, :] = jnp.zeros((8, a_w), F32)
        kbuf[0:WINDOW, :] = jnp.zeros((WINDOW, kv_w), BF16)
        vbuf[0:WINDOW, :] = jnp.zeros((WINDOW, kv_w), BF16)

    x = x_ref[...]
    h = _rms(x, gpre_ref[...]).astype(BF16)

    pa = _dot(h, w_in_ref[:, 0:off_q])
    z = pa[:, 2 * a_w:3 * a_w] * pa[:, 0:a_w]
    zbuf[8:8 + tm, :] = z
    za = (caw_ref[0:1, :] * zbuf[6:6 + tm, :] + caw_ref[1:2, :] * zbuf[7:7 + tm, :]
          + caw_ref[2:3, :] * z)
    ya = pa[:, a_w:2 * a_w] * za
    tail = z[tm - 2:tm, :]
    ca_ref[...] = tail
    zbuf[6:8, :] = tail

    pq = _dot(h, w_in_ref[:, off_q:off_c])
    q = (pq[:, 0:b_w] * (HEAD_DIM ** -0.5)).astype(BF16)
    k = pq[:, b_w:b_w + kv_w]
    v = pq[:, b_w + kv_w:b_w + 2 * kv_w]
    kn_ref[...] = k[tm - WINDOW:tm, :]
    vn_ref[...] = v[tm - WINDOW:tm, :]
    kbuf[WINDOW:WINDOW + tm, :] = k.astype(BF16)
    vbuf[WINDOW:WINDOW + tm, :] = v.astype(BF16)

    rows = Q_PER_KV * WINDOW
    qi_pos = _mod(lax.broadcasted_iota(jnp.int32, (rows, 2 * WINDOW), 0), WINDOW)
    kj_pos = lax.broadcasted_iota(jnp.int32, (rows, 2 * WINDOW), 1)
    diff = qi_pos + WINDOW - kj_pos
    band = (diff >= 0) & (diff < WINDOW)
    band_first = band & (kj_pos + jnp.minimum(t, 1) * WINDOW >= WINDOW)
    sk = [_sink_column(sinks_ref, hh, WINDOW) for hh in range(N_KV)]

    yb_blocks = []
    for i in range(nb):
        qi = q[i * WINDOW:(i + 1) * WINDOW, :]
        kk = kbuf[i * WINDOW:(i + 2) * WINDOW, :]
        vv = vbuf[i * WINDOW:(i + 2) * WINDOW, :]
        mask = band_first if i == 0 else band
        outs = []
        for hh in range(N_KV):
            s = _dot_nt(_stack_heads(qi, hh), kk[:, hh * HEAD_DIM:(hh + 1) * HEAD_DIM])
            s = jnp.where(mask, s, NEG)
            o = _softmax_pv(s, sk[hh], vv[:, hh * HEAD_DIM:(hh + 1) * HEAD_DIM])
            outs += [o[g * WINDOW:(g + 1) * WINDOW, :] for g in range(Q_PER_KV)]
        yb_blocks.append(jnp.concatenate(outs, axis=1))
    yb = jnp.concatenate(yb_blocks, axis=0)
    kbuf[0:WINDOW, :] = kbuf[tm:tm + WINDOW, :]
    vbuf[0:WINDOW, :] = vbuf[tm:tm + WINDOW, :]

    pc = _dot(h, w_in_ref[:, off_c:off_g])
    c_u = pc[:, 0:c_w]
    vn = _rms(pc[:, c_w:2 * c_w], gv_ref[...])
    tri = (lax.broadcasted_iota(jnp.int32, (CHUNK, CHUNK), 0)
           >= lax.broadcasted_iota(jnp.int32, (CHUNK, CHUNK), 1))
    wt = [jnp.where(tri, ws_ref[g], 0.0).astype(BF16) for g in range(C_GROUPS)]
    lane_grp = _div(lax.broadcasted_iota(jnp.int32, (CHUNK, c_w), 1), c_w // C_GROUPS)
    yc_blocks = []
    for i in range(tm // CHUNK):
        vnb = vn[i * CHUNK:(i + 1) * CHUNK, :].astype(BF16)
        mixed = _dot(wt[C_GROUPS - 1], vnb)
        for g in range(C_GROUPS - 2, -1, -1):
            mixed = jnp.where(lane_grp == g, _dot(wt[g], vnb), mixed)
        yc_blocks.append(c_u[i * CHUNK:(i + 1) * CHUNK, :] * (mixed + bs_ref[...]))
    yc = jnp.concatenate(yc_blocks, axis=0)

    merged = _gates(h, w_in_ref, off_g, d_model, 0) * _dot(ya.astype(BF16), w_ba_ref[...])
    merged += _gates(h, w_in_ref, off_g, d_model, 1) * _dot(yb.astype(BF16), w_bb_ref[...])
    merged += _gates(h, w_in_ref, off_g, d_model, 2) * _dot(yc.astype(BF16), w_bc_ref[...])
    out = _dot(merged.astype(BF16), w_o_ref[...])
    xo_ref[...] = x + _rms(out, gpost_ref[...])


def _vmem_spec():
    return pl.BlockSpec(memory_space=pltpu.VMEM)


def _smem_spec():
    return pl.BlockSpec(memory_space=pltpu.SMEM)


def _prompt_mixer(x, lw):
    bsz, seq, d_model = x.shape
    tm = PROMPT_TILE
    a_w = lw["caw"].shape[1]
    c_w = lw["gv"].shape[1]
    kv_w = N_KV * HEAD_DIM
    kern = functools.partial(_prompt_mixer_kernel, tm=tm, d_model=d_model, a_w=a_w, c_w=c_w)
    tile = pl.BlockSpec((None, tm, d_model), lambda b, t: (b, t, 0))
    per_seq = lambda r, c: pl.BlockSpec((None, r, c), lambda b, t: (b, 0, 0))
    return pl.pallas_call(
        kern,
        grid=(bsz, seq // tm),
        in_specs=[tile, _vmem_spec(), _vmem_spec(), _smem_spec()] + [_vmem_spec()] * 9,
        out_specs=[tile, per_seq(2, a_w), per_seq(WINDOW, kv_w), per_seq(WINDOW, kv_w)],
        out_shape=[jax.ShapeDtypeStruct((bsz, seq, d_model), F32),
                   jax.ShapeDtypeStruct((bsz, 2, a_w), F32),
                   jax.ShapeDtypeStruct((bsz, WINDOW, kv_w), F32),
                   jax.ShapeDtypeStruct((bsz, WINDOW, kv_w), F32)],
        scratch_shapes=[pltpu.VMEM((tm + 8, a_w), F32),
                        pltpu.VMEM((tm + WINDOW, kv_w), BF16),
                        pltpu.VMEM((tm + WINDOW, kv_w), BF16)],
        compiler_params=pltpu.CompilerParams(
            dimension_semantics=("arbitrary", "arbitrary"), vmem_limit_bytes=VMEM_LIMIT_BYTES),
        name="prompt_mixer",
    )(x, lw["w_in"], lw["caw"], lw["sinks"], lw["ws"], lw["bs_full"], lw["gv"],
      lw["w_ba"], lw["w_bb"], lw["w_bc"], lw["w_o"], lw["gpre"], lw["gpost"])


def _ffn_columns(d_ff):
    width = 256
    assert d_ff % width == 0
    return width, d_ff // width


def _prompt_ffn_kernel(x_ref, w_up_ref, cfw_ref, cfb_ref, w_down_ref, gpre_ref, gpost_ref,
                       xo_ref, fn_ref, ubuf, act, *, tm, d_ff):
    t = pl.program_id(1)

    @pl.when(t == 0)
    def _():
        ubuf[0:8, :] = jnp.zeros((8, 2 * d_ff), F32)

    x = x_ref[...]
    h = _rms(x, gpre_ref[...]).astype(BF16)
    width, n_col = _ffn_columns(d_ff)

    def conv(lo):
        cols = slice(lo, lo + width)
        u = _dot(h, w_up_ref[:, cols])
        ubuf[8:8 + tm, cols] = u
        return (cfw_ref[0:1, cols] * ubuf[6:6 + tm, cols] + cfw_ref[1:2, cols] * ubuf[7:7 + tm, cols]
                + cfw_ref[2:3, cols] * u + cfb_ref[:, cols])

    for j in range(n_col):
        fa = conv(j * width)
        fb = conv(d_ff + j * width)
        act[:, j * width:(j + 1) * width] = (fa * _sigmoid(fa) * fb).astype(BF16)
    tail = ubuf[tm + 6:tm + 8, :]
    fn_ref[...] = tail
    ubuf[6:8, :] = tail
    out = _dot(act[...], w_down_ref[...])
    xo_ref[...] = x + _rms(out, gpost_ref[...])


def _prompt_ffn(x, lw):
    bsz, seq, d_model = x.shape
    tm = PROMPT_TILE
    d_ff = lw["w_down"].shape[0]
    kern = functools.partial(_prompt_ffn_kernel, tm=tm, d_ff=d_ff)
    tile = pl.BlockSpec((None, tm, d_model), lambda b, t: (b, t, 0))
    return pl.pallas_call(
        kern,
        grid=(bsz, seq // tm),
        in_specs=[tile] + [_vmem_spec()] * 6,
        out_specs=[tile, pl.BlockSpec((None, 2, 2 * d_ff), lambda b, t: (b, 0, 0))],
        out_shape=[jax.ShapeDtypeStruct((bsz, seq, d_model), F32),
                   jax.ShapeDtypeStruct((bsz, 2, 2 * d_ff), F32)],
        scratch_shapes=[pltpu.VMEM((tm + 8, 2 * d_ff), F32),
                        pltpu.VMEM((tm, d_ff), BF16)],
        compiler_params=pltpu.CompilerParams(
            dimension_semantics=("arbitrary", "arbitrary"), vmem_limit_bytes=VMEM_LIMIT_BYTES),
        name="prompt_ffn",
    )(x, lw["w_up"], lw["cfw"], lw["cfb"], lw["w_down"], lw["gpre_f"], lw["gpost_f"])


def _time_steps(ref, n_seq, steps, cols=slice(None)):
    return [ref[t * n_seq:(t + 1) * n_seq, cols] for t in range(steps)]


def _sample_pre_kernel(x_ref, st_ref, w_in_ref, caw_ref, wexp_ref, bexp_ref, gv_ref,
                       w_ba_ref, w_bc_ref, gpre_ref,
                       q_ref, kn_ref, vn_ref, mac_ref, g1_ref, ca_ref, cv_ref,
                       *, n_seq, steps, d_model, a_w, c_w):
    b_w = N_KV * Q_PER_KV * HEAD_DIM
    kv_w = N_KV * HEAD_DIM
    off_q = 3 * a_w
    off_c = off_q + b_w + 2 * kv_w
    off_g = off_c + 2 * c_w
    h = _rms(x_ref[...], gpre_ref[...]).astype(BF16)
    step = lambda a, t: a[t * n_seq:(t + 1) * n_seq, :]

    pa = _dot(h, w_in_ref[:, 0:off_q])
    z = pa[:, 2 * a_w:3 * a_w] * pa[:, 0:a_w]
    hist = _time_steps(st_ref, n_seq, 2) + [step(z, t) for t in range(steps)]
    za = jnp.concatenate(
        [caw_ref[0:1, :] * hist[t] + caw_ref[1:2, :] * hist[t + 1] + caw_ref[2:3, :] * hist[t + 2]
         for t in range(steps)], axis=0)
    ya = pa[:, a_w:2 * a_w] * za
    for i in range(2):
        ca_ref[i * n_seq:(i + 1) * n_seq, :] = hist[steps + i]

    pq = _dot(h, w_in_ref[:, off_q:off_c])
    q_ref[...] = pq[:, 0:b_w]
    kn_ref[...] = pq[:, b_w:b_w + kv_w]
    vn_ref[...] = pq[:, b_w + kv_w:b_w + 2 * kv_w]

    pc = _dot(h, w_in_ref[:, off_c:off_g])
    vn = _rms(pc[:, c_w:2 * c_w], gv_ref[...])
    cv_ref[...] = vn
    mixed = []
    for t in range(steps):
        acc = bexp_ref[t:t + 1, :] + wexp_ref[t * steps:t * steps + 1, :] * step(vn, 0)
        for s in range(1, t + 1):
            acc += wexp_ref[t * steps + s:t * steps + s + 1, :] * step(vn, s)
        mixed.append(acc)
    yc = pc[:, 0:c_w] * jnp.concatenate(mixed, axis=0)

    mac_ref[...] = (_gates(h, w_in_ref, off_g, d_model, 0) * _dot(ya.astype(BF16), w_ba_ref[...])
                    + _gates(h, w_in_ref, off_g, d_model, 2) * _dot(yc.astype(BF16), w_bc_ref[...]))
    g1_ref[...] = _gates(h, w_in_ref, off_g, d_model, 1)


def _sample_pre(x2d, st2d, lw, n_seq, steps):
    rows, d_model = x2d.shape
    a_w = lw["caw"].shape[1]
    c_w = lw["gv"].shape[1]
    b_w = N_KV * Q_PER_KV * HEAD_DIM
    kv_w = N_KV * HEAD_DIM
    kern = functools.partial(_sample_pre_kernel, n_seq=n_seq, steps=steps, d_model=d_model,
                             a_w=a_w, c_w=c_w)
    return pl.pallas_call(
        kern,
        in_specs=[_vmem_spec()] * 10,
        out_specs=[_vmem_spec()] * 7,
        out_shape=[jax.ShapeDtypeStruct((rows, b_w), F32),
                   jax.ShapeDtypeStruct((rows, kv_w), F32),
                   jax.ShapeDtypeStruct((rows, kv_w), F32),
                   jax.ShapeDtypeStruct((rows, d_model), F32),
                   jax.ShapeDtypeStruct((rows, d_model), F32),
                   jax.ShapeDtypeStruct((2 * n_seq, a_w), F32),
                   jax.ShapeDtypeStruct((rows, c_w), F32)],
        compiler_params=pltpu.CompilerParams(vmem_limit_bytes=VMEM_LIMIT_BYTES),
        name="sample_pre",
    )(x2d, st2d, lw["w_in"], lw["caw"], lw["wexp"], lw["bexp"], lw["gv"],
      lw["w_ba"], lw["w_bc"], lw["gpre"])


def _sample_attn_kernel(q_ref, kn_ref, vn_ref, ck_ref, cv_ref, sinks_ref,
                        yb_ref, cko_ref, cvo_ref, *, steps, grp, cache_rows):
    nq = steps * grp
    nc = grp * cache_rows
    b_w = N_KV * Q_PER_KV * HEAD_DIM
    kv_w = N_KV * HEAD_DIM
    q = q_ref[...].reshape(nq, b_w)
    kn = kn_ref[...].reshape(nq, kv_w)
    vn = vn_ref[...].reshape(nq, kv_w)
    kk = jnp.concatenate([ck_ref[...], kn], axis=0).astype(BF16)
    vv = jnp.concatenate([cv_ref[...], vn], axis=0).astype(BF16)

    rows = Q_PER_KV * nq
    shape = (rows, nc + nq)
    r = _mod(lax.broadcasted_iota(jnp.int32, shape, 0), nq)
    c = lax.broadcasted_iota(jnp.int32, shape, 1)
    r_seq, r_t = _mod(r, grp), _div(r, grp)
    in_cache = (_div(c, cache_rows) == r_seq) & (r_t + cache_rows - _mod(c, cache_rows) < WINDOW)
    c_new = jnp.maximum(c - nc, 0)
    in_new = (_mod(c_new, grp) == r_seq) & (_div(c_new, grp) <= r_t)
    mask = ((c < nc) & in_cache) | ((c >= nc) & in_new)

    q = (q * (HEAD_DIM ** -0.5)).astype(BF16)
    outs = []
    for hh in range(N_KV):
        s = _dot_nt(_stack_heads(q, hh), kk[:, hh * HEAD_DIM:(hh + 1) * HEAD_DIM])
        s = jnp.where(mask, s, NEG)
        o = _softmax_pv(s, _sink_column(sinks_ref, hh, nq), vv[:, hh * HEAD_DIM:(hh + 1) * HEAD_DIM])
        outs += [o[g * nq:(g + 1) * nq, :] for g in range(Q_PER_KV)]
    yb_ref[...] = jnp.concatenate(outs, axis=1).reshape(steps, grp, b_w)

    keep = cache_rows - steps
    for new, old, out in ((kn, ck_ref, cko_ref), (vn, cv_ref, cvo_ref)):
        for j in range(grp):
            out[pl.ds(j * cache_rows, keep), :] = old[pl.ds(j * cache_rows + steps, keep), :]
        for t in range(steps):
            out[pl.ds(keep + t, grp, stride=cache_rows), :] = new[t * grp:(t + 1) * grp, :]


def _sample_attn(q, kn, vn, ck2d, cv2d, sinks, n_seq, steps):
    grp = SAMPLE_SEQ_GROUP
    cache_rows = ck2d.shape[0] // n_seq
    b_w = N_KV * Q_PER_KV * HEAD_DIM
    kv_w = N_KV * HEAD_DIM
    kern = functools.partial(_sample_attn_kernel, steps=steps, grp=grp, cache_rows=cache_rows)
    tm_spec = lambda w: pl.BlockSpec((steps, grp, w), lambda i: (0, i, 0))
    cache_spec = pl.BlockSpec((grp * cache_rows, kv_w), lambda i: (i, 0))
    return pl.pallas_call(
        kern,
        grid=(n_seq // grp,),
        in_specs=[tm_spec(b_w), tm_spec(kv_w), tm_spec(kv_w), cache_spec, cache_spec, _smem_spec()],
        out_specs=[tm_spec(b_w), cache_spec, cache_spec],
        out_shape=[jax.ShapeDtypeStruct((steps, n_seq, b_w), F32),
                   jax.ShapeDtypeStruct(ck2d.shape, F32),
                   jax.ShapeDtypeStruct(cv2d.shape, F32)],
        compiler_params=pltpu.CompilerParams(
            dimension_semantics=("arbitrary",), vmem_limit_bytes=VMEM_LIMIT_BYTES),
        name="sample_attn",
    )(q.reshape(steps, n_seq, b_w), kn.reshape(steps, n_seq, kv_w), vn.reshape(steps, n_seq, kv_w),
      ck2d, cv2d, sinks)


def _sample_post_kernel(x_ref, mac_ref, g1_ref, yb_ref, w_bb_ref, w_o_ref, gpost_ref, xo_ref):
    merged = mac_ref[...] + g1_ref[...] * _dot(yb_ref[...].astype(BF16), w_bb_ref[...])
    out = _dot(merged.astype(BF16), w_o_ref[...])
    xo_ref[...] = x_ref[...] + _rms(out, gpost_ref[...])


def _sample_post(x2d, mac, g1, yb, lw):
    return pl.pallas_call(
        _sample_post_kernel,
        in_specs=[_vmem_spec()] * 7,
        out_specs=_vmem_spec(),
        out_shape=jax.ShapeDtypeStruct(x2d.shape, F32),
        compiler_params=pltpu.CompilerParams(vmem_limit_bytes=VMEM_LIMIT_BYTES),
        name="sample_post",
    )(x2d, mac, g1, yb, lw["w_bb"], lw["w_o"], lw["gpost"])


def _sample_ffn_kernel(x_ref, st_ref, w_up_ref, cfw_ref, cfb_ref, w_down_ref, gpre_ref, gpost_ref,
                       xo_ref, fn_ref, act, *, n_seq, steps, d_ff):
    x = x_ref[...]
    h = _rms(x, gpre_ref[...]).astype(BF16)
    width, n_col = _ffn_columns(d_ff)

    def conv(lo):
        cols = slice(lo, lo + width)
        u = _dot(h, w_up_ref[:, cols])
        hist = (_time_steps(st_ref, n_seq, 2, cols)
                + [u[t * n_seq:(t + 1) * n_seq, :] for t in range(steps)])
        for i in range(2):
            fn_ref[i * n_seq:(i + 1) * n_seq, cols] = hist[steps + i]
        return jnp.concatenate(
            [cfw_ref[0:1, cols] * hist[t] + cfw_ref[1:2, cols] * hist[t + 1]
             + cfw_ref[2:3, cols] * hist[t + 2] + cfb_ref[:, cols] for t in range(steps)], axis=0)

    for j in range(n_col):
        fa = conv(j * width)
        fb = conv(d_ff + j * width)
        act[:, j * width:(j + 1) * width] = (fa * _sigmoid(fa) * fb).astype(BF16)
    out = _dot(act[...], w_down_ref[...])
    xo_ref[...] = x + _rms(out, gpost_ref[...])


def _sample_ffn(x_tm, st2d, lw, n_seq, steps):
    rows, _ = x_tm.shape
    d_ff = lw["w_down"].shape[0]
    kern = functools.partial(_sample_ffn_kernel, n_seq=n_seq, steps=steps, d_ff=d_ff)
    return pl.pallas_call(
        kern,
        in_specs=[_vmem_spec()] * 8,
        out_specs=[_vmem_spec()] * 2,
        out_shape=[jax.ShapeDtypeStruct(x_tm.shape, F32),
                   jax.ShapeDtypeStruct(st2d.shape, F32)],
        scratch_shapes=[pltpu.VMEM((rows, d_ff), BF16)],
        compiler_params=pltpu.CompilerParams(vmem_limit_bytes=VMEM_LIMIT_BYTES),
        name="sample_ffn",
    )(x_tm, st2d, lw["w_up"], lw["cfw"], lw["cfb"], lw["w_down"], lw["gpre_f"], lw["gpost_f"])


def _layer_weights(l, steps, w_in, conv_a_w, attn_sinks, spatial_w, spatial_b, g_v, w_branch_a,
                   w_branch_b, w_branch_c, w_out, g_pre_mix, g_post_mix, g_pre_ffn, g_post_ffn,
                   w_up, conv_ffn_w, conv_ffn_b, w_down):
    gdim = g_v.shape[1] // C_GROUPS
    row = lambda a: a[l][None, :]
    return dict(
        w_in=w_in[l], caw=conv_a_w[l], sinks=attn_sinks[l].reshape(-1), ws=spatial_w[l],
        bs_full=jnp.repeat(spatial_b[l].T, gdim, axis=1),
        wexp=jnp.repeat(spatial_w[l][:, :steps, :steps].reshape(C_GROUPS, steps * steps).T, gdim, axis=1),
        bexp=jnp.repeat(spatial_b[l][:, :steps].T, gdim, axis=1),
        gv=row(g_v), w_ba=w_branch_a[l], w_bb=w_branch_b[l], w_bc=w_branch_c[l], w_o=w_out[l],
        gpre=row(g_pre_mix), gpost=row(g_post_mix), gpre_f=row(g_pre_ffn), gpost_f=row(g_post_ffn),
        w_up=w_up[l], cfw=conv_ffn_w[l], cfb=row(conv_ffn_b), w_down=w_down[l])


def kernel(x_prompt, x_sample, state_conv_a, cache_win_k, cache_win_v, state_ffn_conv, w_in, conv_a_w, attn_sinks, spatial_w, spatial_b, g_v, w_branch_a, w_branch_b, w_branch_c, w_out, g_pre_mix, g_post_mix, g_pre_ffn, g_post_ffn, w_up, conv_ffn_w, conv_ffn_b, w_down):
    depth = w_in.shape[0]
    n_seq, steps, d_model = x_sample.shape
    bsz = x_prompt.shape[0]
    assert x_prompt.shape[1] % PROMPT_TILE == 0 and PROMPT_TILE % WINDOW == 0
    assert steps <= CHUNK and n_seq % SAMPLE_SEQ_GROUP == 0 and cache_win_k.shape[2] == WINDOW
    mm = lambda w: w.astype(BF16)
    w_in, w_branch_a, w_branch_b, w_branch_c, w_out, w_up, w_down = map(
        mm, (w_in, w_branch_a, w_branch_b, w_branch_c, w_out, w_up, w_down))

    time_major = lambda a: jnp.swapaxes(a, -3, -2)
    yp = x_prompt
    ys = time_major(x_sample).reshape(steps * n_seq, d_model)
    st_a = time_major(state_conv_a).reshape(depth, 2 * n_seq, -1)
    st_f = time_major(state_ffn_conv).reshape(depth, 2 * n_seq, -1)
    outs = [[] for _ in range(9)]
    for l in range(depth):
        lw = _layer_weights(l, steps, w_in, conv_a_w, attn_sinks, spatial_w, spatial_b, g_v,
                            w_branch_a, w_branch_b, w_branch_c, w_out, g_pre_mix, g_post_mix,
                            g_pre_ffn, g_post_ffn, w_up, conv_ffn_w, conv_ffn_b, w_down)
        yp, ca_p, wk_p, wv_p = _prompt_mixer(yp, lw)
        yp, ff_p = _prompt_ffn(yp, lw)

        q, kn, vn, mac, g1, ca_s, cv_s = _sample_pre(ys, st_a[l], lw, n_seq, steps)
        yb, wk_s, wv_s = _sample_attn(
            q, kn, vn, cache_win_k[l].reshape(n_seq * WINDOW, -1),
            cache_win_v[l].reshape(n_seq * WINDOW, -1), lw["sinks"], n_seq, steps)
        ys = _sample_post(ys, mac, g1, yb.reshape(steps * n_seq, -1), lw)
        ys, ff_s = _sample_ffn(ys, st_f[l], lw, n_seq, steps)

        kv = (WINDOW, N_KV, HEAD_DIM)
        for dst, val in zip(outs, (
                ca_p, wk_p.reshape(bsz, *kv), wv_p.reshape(bsz, *kv), ff_p,
                ca_s.reshape(2, n_seq, -1), wk_s.reshape(n_seq, *kv), wv_s.reshape(n_seq, *kv),
                ff_s.reshape(2, n_seq, -1), cv_s.reshape(steps, n_seq, -1))):
            dst.append(val)
    outs = [jnp.stack(o) for o in outs]
    for i in (4, 7, 8):
        outs[i] = time_major(outs[i])
    return (yp, time_major(ys.reshape(steps, n_seq, d_model))) + tuple(outs)
```

```python
import functools

import jax
import jax.numpy as jnp
from jax import lax
from jax.experimental import pallas as pl
from jax.experimental.pallas import tpu as pltpu

F32 = jnp.float32
BF16 = jnp.bfloat16

HEAD_DIM = 64
N_KV = 2
Q_PER_KV = 4
WINDOW = 128
CHUNK = 128
C_GROUPS = 4
EPS = 1e-6
NEG = -1e30

VMEM_LIMIT_BYTES = 56 * 1024 * 1024
PROMPT_TILE = 256
SAMPLE_SEQ_GROUP = 8


def _rms(x, g):
    return x * lax.rsqrt(jnp.mean(x * x, axis=-1, keepdims=True) + EPS) * g


def _sigmoid(x):
    return 1.0 / (1.0 + jnp.exp(-x))


def _dot(a, b):
    return jnp.dot(a, b, preferred_element_type=F32)


def _dot_nt(a, b):
    return lax.dot_general(a, b, (((1,), (1,)), ((), ())), preferred_element_type=F32)


def _log2(n):
    assert n > 0 and n & (n - 1) == 0, f"{n} must be a power of two"
    return n.bit_length() - 1


def _div(a, n):
    return lax.shift_right_logical(a, _log2(n))


def _mod(a, n):
    return a & ((1 << _log2(n)) - 1)


def _softmax_pv(s, sk, v):
    m = jnp.maximum(jnp.max(s, axis=-1, keepdims=True), sk)
    p = jnp.exp(s - m)
    den = jnp.sum(p, axis=-1, keepdims=True) + jnp.exp(sk - m)
    return _dot(p.astype(BF16), v) * (1.0 / den)


def _sink_column(sinks_ref, kv_head, rows_per_head):
    n = Q_PER_KV * rows_per_head
    grp = _div(lax.broadcasted_iota(jnp.int32, (n, 1), 0), rows_per_head)
    col = jnp.full((n, 1), sinks_ref[kv_head * Q_PER_KV + Q_PER_KV - 1], F32)
    for g in range(Q_PER_KV - 2, -1, -1):
        col = jnp.where(grp == g, sinks_ref[kv_head * Q_PER_KV + g], col)
    return col


def _stack_heads(q, kv_head):
    base = kv_head * Q_PER_KV
    return jnp.concatenate(
        [q[:, (base + g) * HEAD_DIM:(base + g + 1) * HEAD_DIM] for g in range(Q_PER_KV)], axis=0)


def _gates(h, w_in_ref, gate_off, d_model, branch):
    lo = gate_off + branch * d_model
    return _sigmoid(_dot(h, w_in_ref[:, lo:lo + d_model]))


def _prompt_mixer_kernel(x_ref, w_in_ref, caw_ref, sinks_ref, ws_ref, bs_ref, gv_ref,
                         w_ba_ref, w_bb_ref, w_bc_ref, w_o_ref, gpre_ref, gpost_ref,
                         xo_ref, ca_ref, kn_ref, vn_ref,
                         zbuf, kbuf, vbuf, *, tm, d_model, a_w, c_w):
    t = pl.program_id(1)
    b_w = N_KV * Q_PER_KV * HEAD_DIM
    kv_w = N_KV * HEAD_DIM
    nb = tm // WINDOW
    off_q = 3 * a_w
    off_c = off_q + b_w + 2 * kv_w
    off_g = off_c + 2 * c_w

    @pl.when(t == 0)
    def _():
        zbuf[0:8, :] = jnp.zeros((8, a_w), F32)
        kbuf[0:WINDOW, :] = jnp.zeros((WINDOW, kv_w), BF16)
        vbuf[0:WINDOW, :] = jnp.zeros((WINDOW, kv_w), BF16)

    x = x_ref[...]
    h = _rms(x, gpre_ref[...]).astype(BF16)

    pa = _dot(h, w_in_ref[:, 0:off_q])
    z = pa[:, 2 * a_w:3 * a_w] * pa[:, 0:a_w]
    zbuf[8:8 + tm, :] = z
    za = (caw_ref[0:1, :] * zbuf[6:6 + tm, :] + caw_ref[1:2, :] * zbuf[7:7 + tm, :]
          + caw_ref[2:3, :] * z)
    ya = pa[:, a_w:2 * a_w] * za
    tail = z[tm - 2:tm, :]
    ca_ref[...] = tail
    zbuf[6:8, :] = tail

    pq = _dot(h, w_in_ref[:, off_q:off_c])
    q = (pq[:, 0:b_w] * (HEAD_DIM ** -0.5)).astype(BF16)
    k = pq[:, b_w:b_w + kv_w]
    v = pq[:, b_w + kv_w:b_w + 2 * kv_w]
    kn_ref[...] = k[tm - WINDOW:tm, :].T
    vn_ref[...] = v[tm - WINDOW:tm, :].T
    kbuf[WINDOW:WINDOW + tm, :] = k.astype(BF16)
    vbuf[WINDOW:WINDOW + tm, :] = v.astype(BF16)

    rows = Q_PER_KV * WINDOW
    qi_pos = _mod(lax.broadcasted_iota(jnp.int32, (rows, 2 * WINDOW), 0), WINDOW)
    kj_pos = lax.broadcasted_iota(jnp.int32, (rows, 2 * WINDOW), 1)
    diff = qi_pos + WINDOW - kj_pos
    band = (diff >= 0) & (diff < WINDOW)
    band_first = band & (kj_pos + jnp.minimum(t, 1) * WINDOW >= WINDOW)
    sk = [_sink_column(sinks_ref, hh, WINDOW) for hh in range(N_KV)]

    yb_blocks = []
    for i in range(nb):
        qi = q[i * WINDOW:(i + 1) * WINDOW, :]
        kk = kbuf[i * WINDOW:(i + 2) * WINDOW, :]
        vv = vbuf[i * WINDOW:(i + 2) * WINDOW, :]
        mask = band_first if i == 0 else band
        outs = []
        for hh in range(N_KV):
            s = _dot_nt(_stack_heads(qi, hh), kk[:, hh * HEAD_DIM:(hh + 1) * HEAD_DIM])
            s = jnp.where(mask, s, NEG)
            o = _softmax_pv(s, sk[hh], vv[:, hh * HEAD_DIM:(hh + 1) * HEAD_DIM])
            outs += [o[g * WINDOW:(g + 1) * WINDOW, :] for g in range(Q_PER_KV)]
        yb_blocks.append(jnp.concatenate(outs, axis=1))
    yb = jnp.concatenate(yb_blocks, axis=0)
    kbuf[0:WINDOW, :] = kbuf[tm:tm + WINDOW, :]
    vbuf[0:WINDOW, :] = vbuf[tm:tm + WINDOW, :]

    pc = _dot(h, w_in_ref[:, off_c:off_g])
    c_u = pc[:, 0:c_w]
    vn = _rms(pc[:, c_w:2 * c_w], gv_ref[...])
    tri = (lax.broadcasted_iota(jnp.int32, (CHUNK, CHUNK), 0)
           >= lax.broadcasted_iota(jnp.int32, (CHUNK, CHUNK), 1))
    wt = [jnp.where(tri, ws_ref[g], 0.0).astype(BF16) for g in range(C_GROUPS)]
    lane_grp = _div(lax.broadcasted_iota(jnp.int32, (CHUNK, c_w), 1), c_w // C_GROUPS)
    yc_blocks = []
    for i in range(tm // CHUNK):
        vnb = vn[i * CHUNK:(i + 1) * CHUNK, :].astype(BF16)
        mixed = _dot(wt[C_GROUPS - 1], vnb)
        for g in range(C_GROUPS - 2, -1, -1):
            mixed = jnp.where(lane_grp == g, _dot(wt[g], vnb), mixed)
        yc_blocks.append(c_u[i * CHUNK:(i + 1) * CHUNK, :] * (mixed + bs_ref[...]))
    yc = jnp.concatenate(yc_blocks, axis=0)

    merged = _gates(h, w_in_ref, off_g, d_model, 0) * _dot(ya.astype(BF16), w_ba_ref[...])
    merged += _gates(h, w_in_ref, off_g, d_model, 1) * _dot(yb.astype(BF16), w_bb_ref[...])
    merged += _gates(h, w_in_ref, off_g, d_model, 2) * _dot(yc.astype(BF16), w_bc_ref[...])
    out = _dot(merged.astype(BF16), w_o_ref[...])
    xo_ref[...] = x + _rms(out, gpost_ref[...])


def _vmem_spec():
    return pl.BlockSpec(memory_space=pltpu.VMEM)


def _smem_spec():
    return pl.BlockSpec(memory_space=pltpu.SMEM)


def _prompt_mixer(x, lw):
    bsz, seq, d_model = x.shape
    tm = PROMPT_TILE
    a_w = lw["caw"].shape[1]
    c_w = lw["gv"].shape[1]
    kv_w = N_KV * HEAD_DIM
    kern = functools.partial(_prompt_mixer_kernel, tm=tm, d_model=d_model, a_w=a_w, c_w=c_w)
    tile = pl.BlockSpec((None, tm, d_model), lambda b, t: (b, t, 0))
    per_seq = lambda r, c: pl.BlockSpec((None, r, c), lambda b, t: (b, 0, 0))
    return pl.pallas_call(
        kern,
        grid=(bsz, seq // tm),
        in_specs=[tile, _vmem_spec(), _vmem_spec(), _smem_spec()] + [_vmem_spec()] * 9,
        out_specs=[tile, per_seq(2, a_w), per_seq(kv_w, WINDOW), per_seq(kv_w, WINDOW)],
        out_shape=[jax.ShapeDtypeStruct((bsz, seq, d_model), F32),
                   jax.ShapeDtypeStruct((bsz, 2, a_w), F32),
                   jax.ShapeDtypeStruct((bsz, kv_w, WINDOW), F32),
                   jax.ShapeDtypeStruct((bsz, kv_w, WINDOW), F32)],
        scratch_shapes=[pltpu.VMEM((tm + 8, a_w), F32),
                        pltpu.VMEM((tm + WINDOW, kv_w), BF16),
                        pltpu.VMEM((tm + WINDOW, kv_w), BF16)],
        compiler_params=pltpu.CompilerParams(
            dimension_semantics=("arbitrary", "arbitrary"), vmem_limit_bytes=VMEM_LIMIT_BYTES),
        name="prompt_mixer",
    )(x, lw["w_in"], lw["caw"], lw["sinks"], lw["ws"], lw["bs_full"], lw["gv"],
      lw["w_ba"], lw["w_bb"], lw["w_bc"], lw["w_o"], lw["gpre"], lw["gpost"])


def _ffn_columns(d_ff):
    width = 256
    assert d_ff % width == 0
    return width, d_ff // width


def _prompt_ffn_kernel(x_ref, w_up_ref, cfw_ref, cfb_ref, w_down_ref, gpre_ref, gpost_ref,
                       xo_ref, fn_ref, ubuf, act, *, tm, d_ff):
    t = pl.program_id(1)

    @pl.when(t == 0)
    def _():
        ubuf[0:8, :] = jnp.zeros((8, 2 * d_ff), F32)

    x = x_ref[...]
    h = _rms(x, gpre_ref[...]).astype(BF16)
    width, n_col = _ffn_columns(d_ff)

    def conv(lo):
        cols = slice(lo, lo + width)
        u = _dot(h, w_up_ref[:, cols])
        ubuf[8:8 + tm, cols] = u
        return (cfw_ref[0:1, cols] * ubuf[6:6 + tm, cols] + cfw_ref[1:2, cols] * ubuf[7:7 + tm, cols]
                + cfw_ref[2:3, cols] * u + cfb_ref[:, cols])

    for j in range(n_col):
        fa = conv(j * width)
        fb = conv(d_ff + j * width)
        act[:, j * width:(j + 1) * width] = (fa * _sigmoid(fa) * fb).astype(BF16)
    tail = ubuf[tm + 6:tm + 8, :]
    fn_ref[...] = tail
    ubuf[6:8, :] = tail
    out = _dot(act[...], w_down_ref[...])
    xo_ref[...] = x + _rms(out, gpost_ref[...])


def _prompt_ffn(x, lw):
    bsz, seq, d_model = x.shape
    tm = PROMPT_TILE
    d_ff = lw["w_down"].shape[0]
    kern = functools.partial(_prompt_ffn_kernel, tm=tm, d_ff=d_ff)
    tile = pl.BlockSpec((None, tm, d_model), lambda b, t: (b, t, 0))
    return pl.pallas_call(
        kern,
        grid=(bsz, seq // tm),
        in_specs=[tile] + [_vmem_spec()] * 6,
        out_specs=[tile, pl.BlockSpec((None, 2, 2 * d_ff), lambda b, t: (b, 0, 0))],
        out_shape=[jax.ShapeDtypeStruct((bsz, seq, d_model), F32),
                   jax.ShapeDtypeStruct((bsz, 2, 2 * d_ff), F32)],
        scratch_shapes=[pltpu.VMEM((tm + 8, 2 * d_ff), F32),
                        pltpu.VMEM((tm, d_ff), BF16)],
        compiler_params=pltpu.CompilerParams(
            dimension_semantics=("arbitrary", "arbitrary"), vmem_limit_bytes=VMEM_LIMIT_BYTES),
        name="prompt_ffn",
    )(x, lw["w_up"], lw["cfw"], lw["cfb"], lw["w_down"], lw["gpre_f"], lw["gpost_f"])


def _time_steps(ref, n_seq, steps, cols=slice(None)):
    return [ref[t * n_seq:(t + 1) * n_seq, cols] for t in range(steps)]


def _sample_pre_kernel(x_ref, st_ref, w_in_ref, caw_ref, wexp_ref, bexp_ref, gv_ref,
                       w_ba_ref, w_bc_ref, gpre_ref,
                       q_ref, kn_ref, vn_ref, mac_ref, g1_ref, ca_ref, cv_ref,
                       *, n_seq, steps, d_model, a_w, c_w):
    b_w = N_KV * Q_PER_KV * HEAD_DIM
    kv_w = N_KV * HEAD_DIM
    off_q = 3 * a_w
    off_c = off_q + b_w + 2 * kv_w
    off_g = off_c + 2 * c_w
    h = _rms(x_ref[...], gpre_ref[...]).astype(BF16)
    step = lambda a, t: a[t * n_seq:(t + 1) * n_seq, :]

    pa = _dot(h, w_in_ref[:, 0:off_q])
    z = pa[:, 2 * a_w:3 * a_w] * pa[:, 0:a_w]
    hist = _time_steps(st_ref, n_seq, 2) + [step(z, t) for t in range(steps)]
    za = jnp.concatenate(
        [caw_ref[0:1, :] * hist[t] + caw_ref[1:2, :] * hist[t + 1] + caw_ref[2:3, :] * hist[t + 2]
         for t in range(steps)], axis=0)
    ya = pa[:, a_w:2 * a_w] * za
    for i in range(2):
        ca_ref[i * n_seq:(i + 1) * n_seq, :] = hist[steps + i]

    pq = _dot(h, w_in_ref[:, off_q:off_c])
    q_ref[...] = pq[:, 0:b_w]
    kn_ref[...] = pq[:, b_w:b_w + kv_w]
    vn_ref[...] = pq[:, b_w + kv_w:b_w + 2 * kv_w]

    pc = _dot(h, w_in_ref[:, off_c:off_g])
    vn = _rms(pc[:, c_w:2 * c_w], gv_ref[...])
    cv_ref[...] = vn
    mixed = []
    for t in range(steps):
        acc = bexp_ref[t:t + 1, :] + wexp_ref[t * steps:t * steps + 1, :] * step(vn, 0)
        for s in range(1, t + 1):
            acc += wexp_ref[t * steps + s:t * steps + s + 1, :] * step(vn, s)
        mixed.append(acc)
    yc = pc[:, 0:c_w] * jnp.concatenate(mixed, axis=0)

    mac_ref[...] = (_gates(h, w_in_ref, off_g, d_model, 0) * _dot(ya.astype(BF16), w_ba_ref[...])
                    + _gates(h, w_in_ref, off_g, d_model, 2) * _dot(yc.astype(BF16), w_bc_ref[...]))
    g1_ref[...] = _gates(h, w_in_ref, off_g, d_model, 1)


def _sample_pre(x2d, st2d, lw, n_seq, steps):
    rows, d_model = x2d.shape
    a_w = lw["caw"].shape[1]
    c_w = lw["gv"].shape[1]
    b_w = N_KV * Q_PER_KV * HEAD_DIM
    kv_w = N_KV * HEAD_DIM
    kern = functools.partial(_sample_pre_kernel, n_seq=n_seq, steps=steps, d_model=d_model,
                             a_w=a_w, c_w=c_w)
    return pl.pallas_call(
        kern,
        in_specs=[_vmem_spec()] * 10,
        out_specs=[_vmem_spec()] * 7,
        out_shape=[jax.ShapeDtypeStruct((rows, b_w), F32),
                   jax.ShapeDtypeStruct((rows, kv_w), F32),
                   jax.ShapeDtypeStruct((rows, kv_w), F32),
                   jax.ShapeDtypeStruct((rows, d_model), F32),
                   jax.ShapeDtypeStruct((rows, d_model), F32),
                   jax.ShapeDtypeStruct((2 * n_seq, a_w), F32),
                   jax.ShapeDtypeStruct((rows, c_w), F32)],
        compiler_params=pltpu.CompilerParams(vmem_limit_bytes=VMEM_LIMIT_BYTES),
        name="sample_pre",
    )(x2d, st2d, lw["w_in"], lw["caw"], lw["wexp"], lw["bexp"], lw["gv"],
      lw["w_ba"], lw["w_bc"], lw["gpre"])


def _sample_attn_kernel(q_ref, kn_ref, vn_ref, ck_ref, cv_ref, sinks_ref,
                        yb_ref, cko_ref, cvo_ref, *, steps, grp, cache_rows):
    nq = steps * grp
    nc = grp * cache_rows
    b_w = N_KV * Q_PER_KV * HEAD_DIM
    kv_w = N_KV * HEAD_DIM
    q = (q_ref[...].reshape(nq, b_w) * (HEAD_DIM ** -0.5)).astype(BF16)
    kn = kn_ref[...].reshape(nq, kv_w)
    vn = vn_ref[...].reshape(nq, kv_w)

    rows = Q_PER_KV * nq
    r = _mod(lax.broadcasted_iota(jnp.int32, (rows, nc), 0), nq)
    c = lax.broadcasted_iota(jnp.int32, (rows, nc), 1)
    in_cache = ((_div(c, cache_rows) == _mod(r, grp))
                & (_div(r, grp) + cache_rows - _mod(c, cache_rows) < WINDOW))
    r = _mod(lax.broadcasted_iota(jnp.int32, (rows, nq), 0), nq)
    c = lax.broadcasted_iota(jnp.int32, (rows, nq), 1)
    in_new = (_mod(c, grp) == _mod(r, grp)) & (_div(c, grp) <= _div(r, grp))

    def head_cache(ref, hh):
        return jnp.concatenate(
            [ref[j * kv_w + hh * HEAD_DIM:j * kv_w + (hh + 1) * HEAD_DIM, :] for j in range(grp)],
            axis=1).astype(BF16)

    outs = []
    for hh in range(N_KV):
        cols = slice(hh * HEAD_DIM, (hh + 1) * HEAD_DIM)
        qs = _stack_heads(q, hh)
        s_c = jnp.where(in_cache, _dot(qs, head_cache(ck_ref, hh)), NEG)
        s_n = jnp.where(in_new, _dot_nt(qs, kn[:, cols].astype(BF16)), NEG)
        sk = _sink_column(sinks_ref, hh, nq)
        m = jnp.maximum(jnp.maximum(jnp.max(s_c, axis=-1, keepdims=True),
                                    jnp.max(s_n, axis=-1, keepdims=True)), sk)
        p_c = jnp.exp(s_c - m)
        p_n = jnp.exp(s_n - m)
        den = (jnp.sum(p_c, axis=-1, keepdims=True) + jnp.sum(p_n, axis=-1, keepdims=True)
               + jnp.exp(sk - m))
        o = (_dot_nt(p_c.astype(BF16), head_cache(cv_ref, hh))
             + _dot(p_n.astype(BF16), vn[:, cols].astype(BF16))) * (1.0 / den)
        outs += [o[g * nq:(g + 1) * nq, :] for g in range(Q_PER_KV)]
    yb_ref[...] = jnp.concatenate(outs, axis=1).reshape(steps, grp, b_w)

    keep = cache_rows - steps
    for new, old, out in ((kn, ck_ref, cko_ref), (vn, cv_ref, cvo_ref)):
        out[:, 0:keep] = old[:, steps:cache_rows]
        new_t = new.T
        for j in range(grp):
            for t in range(steps):
                out[j * kv_w:(j + 1) * kv_w, keep + t:keep + t + 1] = new_t[:, t * grp + j:t * grp + j + 1]


def _sample_attn(q, kn, vn, ck_fm, cv_fm, sinks, n_seq, steps):
    grp = SAMPLE_SEQ_GROUP
    kv_w = N_KV * HEAD_DIM
    b_w = Q_PER_KV * kv_w
    cache_rows = ck_fm.shape[1]
    kern = functools.partial(_sample_attn_kernel, steps=steps, grp=grp, cache_rows=cache_rows)
    tm_spec = lambda w: pl.BlockSpec((steps, grp, w), lambda i: (0, i, 0))
    cache_spec = pl.BlockSpec((grp * kv_w, cache_rows), lambda i: (i, 0))
    return pl.pallas_call(
        kern,
        grid=(n_seq // grp,),
        in_specs=[tm_spec(b_w), tm_spec(kv_w), tm_spec(kv_w), cache_spec, cache_spec, _smem_spec()],
        out_specs=[tm_spec(b_w), cache_spec, cache_spec],
        out_shape=[jax.ShapeDtypeStruct((steps, n_seq, b_w), F32),
                   jax.ShapeDtypeStruct(ck_fm.shape, F32),
                   jax.ShapeDtypeStruct(cv_fm.shape, F32)],
        compiler_params=pltpu.CompilerParams(
            dimension_semantics=("arbitrary",), vmem_limit_bytes=VMEM_LIMIT_BYTES),
        name="sample_attn",
    )(q.reshape(steps, n_seq, b_w), kn.reshape(steps, n_seq, kv_w), vn.reshape(steps, n_seq, kv_w),
      ck_fm, cv_fm, sinks)


def _sample_post_kernel(x_ref, mac_ref, g1_ref, yb_ref, w_bb_ref, w_o_ref, gpost_ref, xo_ref):
    merged = mac_ref[...] + g1_ref[...] * _dot(yb_ref[...].astype(BF16), w_bb_ref[...])
    out = _dot(merged.astype(BF16), w_o_ref[...])
    xo_ref[...] = x_ref[...] + _rms(out, gpost_ref[...])


def _sample_post(x2d, mac, g1, yb, lw):
    return pl.pallas_call(
        _sample_post_kernel,
        in_specs=[_vmem_spec()] * 7,
        out_specs=_vmem_spec(),
        out_shape=jax.ShapeDtypeStruct(x2d.shape, F32),
        compiler_params=pltpu.CompilerParams(vmem_limit_bytes=VMEM_LIMIT_BYTES),
        name="sample_post",
    )(x2d, mac, g1, yb, lw["w_bb"], lw["w_o"], lw["gpost"])


def _sample_ffn_kernel(x_ref, st_ref, w_up_ref, cfw_ref, cfb_ref, w_down_ref, gpre_ref, gpost_ref,
                       xo_ref, fn_ref, act, *, n_seq, steps, d_ff):
    x = x_ref[...]
    h = _rms(x, gpre_ref[...]).astype(BF16)
    width, n_col = _ffn_columns(d_ff)

    def conv(lo):
        cols = slice(lo, lo + width)
        u = _dot(h, w_up_ref[:, cols])
        hist = (_time_steps(st_ref, n_seq, 2, cols)
                + [u[t * n_seq:(t + 1) * n_seq, :] for t in range(steps)])
        for i in range(2):
            fn_ref[i * n_seq:(i + 1) * n_seq, cols] = hist[steps + i]
        return jnp.concatenate(
            [cfw_ref[0:1, cols] * hist[t] + cfw_ref[1:2, cols] * hist[t + 1]
             + cfw_ref[2:3, cols] * hist[t + 2] + cfb_ref[:, cols] for t in range(steps)], axis=0)

    for j in range(n_col):
        fa = conv(j * width)
        fb = conv(d_ff + j * width)
        act[:, j * width:(j + 1) * width] = (fa * _sigmoid(fa) * fb).astype(BF16)
    out = _dot(act[...], w_down_ref[...])
    xo_ref[...] = x + _rms(out, gpost_ref[...])


def _sample_ffn(x_tm, st2d, lw, n_seq, steps):
    rows, _ = x_tm.shape
    d_ff = lw["w_down"].shape[0]
    kern = functools.partial(_sample_ffn_kernel, n_seq=n_seq, steps=steps, d_ff=d_ff)
    return pl.pallas_call(
        kern,
        in_specs=[_vmem_spec()] * 8,
        out_specs=[_vmem_spec()] * 2,
        out_shape=[jax.ShapeDtypeStruct(x_tm.shape, F32),
                   jax.ShapeDtypeStruct(st2d.shape, F32)],
        scratch_shapes=[pltpu.VMEM((rows, d_ff), BF16)],
        compiler_params=pltpu.CompilerParams(vmem_limit_bytes=VMEM_LIMIT_BYTES),
        name="sample_ffn",
    )(x_tm, st2d, lw["w_up"], lw["cfw"], lw["cfb"], lw["w_down"], lw["gpre_f"], lw["gpost_f"])


def _layer_weights(l, steps, w_in, conv_a_w, attn_sinks, spatial_w, spatial_b, g_v, w_branch_a,
                   w_branch_b, w_branch_c, w_out, g_pre_mix, g_post_mix, g_pre_ffn, g_post_ffn,
                   w_up, conv_ffn_w, conv_ffn_b, w_down):
    gdim = g_v.shape[1] // C_GROUPS
    row = lambda a: a[l][None, :]
    return dict(
        w_in=w_in[l], caw=conv_a_w[l], sinks=attn_sinks[l].reshape(-1), ws=spatial_w[l],
        bs_full=jnp.repeat(spatial_b[l].T, gdim, axis=1),
        wexp=jnp.repeat(spatial_w[l][:, :steps, :steps].reshape(C_GROUPS, steps * steps).T, gdim, axis=1),
        bexp=jnp.repeat(spatial_b[l][:, :steps].T, gdim, axis=1),
        gv=row(g_v), w_ba=w_branch_a[l], w_bb=w_branch_b[l], w_bc=w_branch_c[l], w_o=w_out[l],
        gpre=row(g_pre_mix), gpost=row(g_post_mix), gpre_f=row(g_pre_ffn), gpost_f=row(g_post_ffn),
        w_up=w_up[l], cfw=conv_ffn_w[l], cfb=row(conv_ffn_b), w_down=w_down[l])


def kernel(x_prompt, x_sample, state_conv_a, cache_win_k, cache_win_v, state_ffn_conv, w_in, conv_a_w, attn_sinks, spatial_w, spatial_b, g_v, w_branch_a, w_branch_b, w_branch_c, w_out, g_pre_mix, g_post_mix, g_pre_ffn, g_post_ffn, w_up, conv_ffn_w, conv_ffn_b, w_down):
    depth = w_in.shape[0]
    n_seq, steps, d_model = x_sample.shape
    bsz = x_prompt.shape[0]
    assert x_prompt.shape[1] % PROMPT_TILE == 0 and PROMPT_TILE % WINDOW == 0
    assert steps <= CHUNK and n_seq % SAMPLE_SEQ_GROUP == 0 and cache_win_k.shape[2] == WINDOW
    mm = lambda w: w.astype(BF16)
    w_in, w_branch_a, w_branch_b, w_branch_c, w_out, w_up, w_down = map(
        mm, (w_in, w_branch_a, w_branch_b, w_branch_c, w_out, w_up, w_down))

    time_major = lambda a: jnp.swapaxes(a, -3, -2)
    yp = x_prompt
    ys = time_major(x_sample).reshape(steps * n_seq, d_model)
    st_a = time_major(state_conv_a).reshape(depth, 2 * n_seq, -1)
    st_f = time_major(state_ffn_conv).reshape(depth, 2 * n_seq, -1)
    kv_w = N_KV * HEAD_DIM
    feature_major = lambda c: jnp.transpose(c, (0, 1, 3, 4, 2)).reshape(depth, n_seq * kv_w, WINDOW)
    window_out = lambda c: jnp.transpose(c.reshape(depth, -1, N_KV, HEAD_DIM, WINDOW), (0, 1, 4, 2, 3))
    ck_fm, cv_fm = feature_major(cache_win_k), feature_major(cache_win_v)
    outs = [[] for _ in range(9)]
    for l in range(depth):
        lw = _layer_weights(l, steps, w_in, conv_a_w, attn_sinks, spatial_w, spatial_b, g_v,
                            w_branch_a, w_branch_b, w_branch_c, w_out, g_pre_mix, g_post_mix,
                            g_pre_ffn, g_post_ffn, w_up, conv_ffn_w, conv_ffn_b, w_down)
        yp, ca_p, wk_p, wv_p = _prompt_mixer(yp, lw)
        yp, ff_p = _prompt_ffn(yp, lw)

        q, kn, vn, mac, g1, ca_s, cv_s = _sample_pre(ys, st_a[l], lw, n_seq, steps)
        yb, wk_s, wv_s = _sample_attn(q, kn, vn, ck_fm[l], cv_fm[l], lw["sinks"], n_seq, steps)
        ys = _sample_post(ys, mac, g1, yb.reshape(steps * n_seq, -1), lw)
        ys, ff_s = _sample_ffn(ys, st_f[l], lw, n_seq, steps)

        for dst, val in zip(outs, (
                ca_p, wk_p, wv_p, ff_p, ca_s.reshape(2, n_seq, -1), wk_s, wv_s,
                ff_s.reshape(2, n_seq, -1), cv_s.reshape(steps, n_seq, -1))):
            dst.append(val)
    outs = [jnp.stack(o) for o in outs]
    for i in (4, 7, 8):
        outs[i] = time_major(outs[i])
    for i in (1, 2, 5, 6):
        outs[i] = window_out(outs[i])
    return (yp, time_major(ys.reshape(steps, n_seq, d_model))) + tuple(outs)
```

```python
import functools

import jax
import jax.numpy as jnp
from jax import lax
from jax.experimental import pallas as pl
from jax.experimental.pallas import tpu as pltpu

F32 = jnp.float32
BF16 = jnp.bfloat16

HEAD_DIM = 64
N_KV = 2
Q_PER_KV = 4
WINDOW = 128
CHUNK = 128
C_GROUPS = 4
EPS = 1e-6
NEG = -1e30

VMEM_LIMIT_BYTES = 56 * 1024 * 1024
PROMPT_TILE = 512
SAMPLE_SEQ_GROUP = 8


def _rms(x, g):
    return x * lax.rsqrt(jnp.mean(x * x, axis=-1, keepdims=True) + EPS) * g


def _sigmoid(x):
    return 1.0 / (1.0 + jnp.exp(-x))


def _dot(a, b):
    return jnp.dot(a, b, preferred_element_type=F32)


def _dot_nt(a, b):
    return lax.dot_general(a, b, (((1,), (1,)), ((), ())), preferred_element_type=F32)


def _log2(n):
    assert n > 0 and n & (n - 1) == 0, f"{n} must be a power of two"
    return n.bit_length() - 1


def _div(a, n):
    return lax.shift_right_logical(a, _log2(n))


def _mod(a, n):
    return a & ((1 << _log2(n)) - 1)


def _softmax_pv(s, sk, v):
    m = jnp.maximum(jnp.max(s, axis=-1, keepdims=True), sk)
    p = jnp.exp(s - m)
    den = jnp.sum(p, axis=-1, keepdims=True) + jnp.exp(sk - m)
    return _dot(p.astype(BF16), v) * (1.0 / den)


def _sink_column(sinks_ref, kv_head, rows_per_head):
    n = Q_PER_KV * rows_per_head
    grp = _div(lax.broadcasted_iota(jnp.int32, (n, 1), 0), rows_per_head)
    col = jnp.full((n, 1), sinks_ref[kv_head * Q_PER_KV + Q_PER_KV - 1], F32)
    for g in range(Q_PER_KV - 2, -1, -1):
        col = jnp.where(grp == g, sinks_ref[kv_head * Q_PER_KV + g], col)
    return col


def _stack_heads(q, kv_head):
    base = kv_head * Q_PER_KV
    return jnp.concatenate(
        [q[:, (base + g) * HEAD_DIM:(base + g + 1) * HEAD_DIM] for g in range(Q_PER_KV)], axis=0)


def _gates(h, w_in_ref, gate_off, d_model, branch):
    lo = gate_off + branch * d_model
    return _sigmoid(_dot(h, w_in_ref[:, lo:lo + d_model]))


def _prompt_mixer_kernel(x_ref, w_in_ref, caw_ref, sinks_ref, ws_ref, bs_ref, gv_ref,
                         w_ba_ref, w_bb_ref, w_bc_ref, w_o_ref, gpre_ref, gpost_ref,
                         xo_ref, ca_ref, kn_ref, vn_ref,
                         zbuf, kbuf, vbuf, *, tm, d_model, a_w, c_w):
    t = pl.program_id(1)
    b_w = N_KV * Q_PER_KV * HEAD_DIM
    kv_w = N_KV * HEAD_DIM
    nb = tm // WINDOW
    off_q = 3 * a_w
    off_c = off_q + b_w + 2 * kv_w
    off_g = off_c + 2 * c_w

    @pl.when(t == 0)
    def _():
        zbuf[0:8, :] = jnp.zeros((8, a_w), F32)
        kbuf[0:WINDOW, :] = jnp.zeros((WINDOW, kv_w), BF16)
        vbuf[0:WINDOW, :] = jnp.zeros((WINDOW, kv_w), BF16)

    x = x_ref[...]
    h = _rms(x, gpre_ref[...]).astype(BF16)

    pa = _dot(h, w_in_ref[:, 0:off_q])
    z = pa[:, 2 * a_w:3 * a_w] * pa[:, 0:a_w]
    zbuf[8:8 + tm, :] = z
    za = (caw_ref[0:1, :] * zbuf[6:6 + tm, :] + caw_ref[1:2, :] * zbuf[7:7 + tm, :]
          + caw_ref[2:3, :] * z)
    ya = pa[:, a_w:2 * a_w] * za
    tail = z[tm - 2:tm, :]
    ca_ref[...] = tail
    zbuf[6:8, :] = tail

    pq = _dot(h, w_in_ref[:, off_q:off_c])
    q = (pq[:, 0:b_w] * (HEAD_DIM ** -0.5)).astype(BF16)
    k = pq[:, b_w:b_w + kv_w]
    v = pq[:, b_w + kv_w:b_w + 2 * kv_w]
    kn_ref[...] = k[tm - WINDOW:tm, :].T
    vn_ref[...] = v[tm - WINDOW:tm, :].T
    kbuf[WINDOW:WINDOW + tm, :] = k.astype(BF16)
    vbuf[WINDOW:WINDOW + tm, :] = v.astype(BF16)

    rows = Q_PER_KV * WINDOW
    qi_pos = _mod(lax.broadcasted_iota(jnp.int32, (rows, 2 * WINDOW), 0), WINDOW)
    kj_pos = lax.broadcasted_iota(jnp.int32, (rows, 2 * WINDOW), 1)
    diff = qi_pos + WINDOW - kj_pos
    band = (diff >= 0) & (diff < WINDOW)
    band_first = band & (kj_pos + jnp.minimum(t, 1) * WINDOW >= WINDOW)
    sk = [_sink_column(sinks_ref, hh, WINDOW) for hh in range(N_KV)]

    yb_blocks = []
    for i in range(nb):
        qi = q[i * WINDOW:(i + 1) * WINDOW, :]
        kk = kbuf[i * WINDOW:(i + 2) * WINDOW, :]
        vv = vbuf[i * WINDOW:(i + 2) * WINDOW, :]
        mask = band_first if i == 0 else band
        outs = []
        for hh in range(N_KV):
            s = _dot_nt(_stack_heads(qi, hh), kk[:, hh * HEAD_DIM:(hh + 1) * HEAD_DIM])
            s = jnp.where(mask, s, NEG)
            o = _softmax_pv(s, sk[hh], vv[:, hh * HEAD_DIM:(hh + 1) * HEAD_DIM])
            outs += [o[g * WINDOW:(g + 1) * WINDOW, :] for g in range(Q_PER_KV)]
        yb_blocks.append(jnp.concatenate(outs, axis=1))
    yb = jnp.concatenate(yb_blocks, axis=0)
    kbuf[0:WINDOW, :] = kbuf[tm:tm + WINDOW, :]
    vbuf[0:WINDOW, :] = vbuf[tm:tm + WINDOW, :]

    pc = _dot(h, w_in_ref[:, off_c:off_g])
    c_u = pc[:, 0:c_w]
    vn = _rms(pc[:, c_w:2 * c_w], gv_ref[...])
    tri = (lax.broadcasted_iota(jnp.int32, (CHUNK, CHUNK), 0)
           >= lax.broadcasted_iota(jnp.int32, (CHUNK, CHUNK), 1))
    wt = [jnp.where(tri, ws_ref[g], 0.0).astype(BF16) for g in range(C_GROUPS)]
    lane_grp = _div(lax.broadcasted_iota(jnp.int32, (CHUNK, c_w), 1), c_w // C_GROUPS)
    yc_blocks = []
    for i in range(tm // CHUNK):
        vnb = vn[i * CHUNK:(i + 1) * CHUNK, :].astype(BF16)
        mixed = _dot(wt[C_GROUPS - 1], vnb)
        for g in range(C_GROUPS - 2, -1, -1):
            mixed = jnp.where(lane_grp == g, _dot(wt[g], vnb), mixed)
        yc_blocks.append(c_u[i * CHUNK:(i + 1) * CHUNK, :] * (mixed + bs_ref[...]))
    yc = jnp.concatenate(yc_blocks, axis=0)

    merged = _gates(h, w_in_ref, off_g, d_model, 0) * _dot(ya.astype(BF16), w_ba_ref[...])
    merged += _gates(h, w_in_ref, off_g, d_model, 1) * _dot(yb.astype(BF16), w_bb_ref[...])
    merged += _gates(h, w_in_ref, off_g, d_model, 2) * _dot(yc.astype(BF16), w_bc_ref[...])
    out = _dot(merged.astype(BF16), w_o_ref[...])
    xo_ref[...] = x + _rms(out, gpost_ref[...])


def _vmem_spec():
    return pl.BlockSpec(memory_space=pltpu.VMEM)


def _smem_spec():
    return pl.BlockSpec(memory_space=pltpu.SMEM)


def _prompt_mixer(x, lw):
    bsz, seq, d_model = x.shape
    tm = PROMPT_TILE
    a_w = lw["caw"].shape[1]
    c_w = lw["gv"].shape[1]
    kv_w = N_KV * HEAD_DIM
    kern = functools.partial(_prompt_mixer_kernel, tm=tm, d_model=d_model, a_w=a_w, c_w=c_w)
    tile = pl.BlockSpec((None, tm, d_model), lambda b, t: (b, t, 0))
    per_seq = lambda r, c: pl.BlockSpec((None, r, c), lambda b, t: (b, 0, 0))
    return pl.pallas_call(
        kern,
        grid=(bsz, seq // tm),
        in_specs=[tile, _vmem_spec(), _vmem_spec(), _smem_spec()] + [_vmem_spec()] * 9,
        out_specs=[tile, per_seq(2, a_w), per_seq(kv_w, WINDOW), per_seq(kv_w, WINDOW)],
        out_shape=[jax.ShapeDtypeStruct((bsz, seq, d_model), F32),
                   jax.ShapeDtypeStruct((bsz, 2, a_w), F32),
                   jax.ShapeDtypeStruct((bsz, kv_w, WINDOW), F32),
                   jax.ShapeDtypeStruct((bsz, kv_w, WINDOW), F32)],
        scratch_shapes=[pltpu.VMEM((tm + 8, a_w), F32),
                        pltpu.VMEM((tm + WINDOW, kv_w), BF16),
                        pltpu.VMEM((tm + WINDOW, kv_w), BF16)],
        compiler_params=pltpu.CompilerParams(
            dimension_semantics=("arbitrary", "arbitrary"), vmem_limit_bytes=VMEM_LIMIT_BYTES),
        name="prompt_mixer",
    )(x, lw["w_in"], lw["caw"], lw["sinks"], lw["ws"], lw["bs_full"], lw["gv"],
      lw["w_ba"], lw["w_bb"], lw["w_bc"], lw["w_o"], lw["gpre"], lw["gpost"])


def _ffn_columns(d_ff):
    width = 256
    assert d_ff % width == 0
    return width, d_ff // width


def _prompt_ffn_kernel(x_ref, w_up_ref, cfw_ref, cfb_ref, w_down_ref, gpre_ref, gpost_ref,
                       xo_ref, fn_ref, ubuf, act, *, tm, d_ff):
    t = pl.program_id(1)

    @pl.when(t == 0)
    def _():
        ubuf[0:8, :] = jnp.zeros((8, 2 * d_ff), F32)

    x = x_ref[...]
    h = _rms(x, gpre_ref[...]).astype(BF16)
    width, n_col = _ffn_columns(d_ff)

    def conv(lo):
        cols = slice(lo, lo + width)
        u = _dot(h, w_up_ref[:, cols])
        ubuf[8:8 + tm, cols] = u
        return (cfw_ref[0:1, cols] * ubuf[6:6 + tm, cols] + cfw_ref[1:2, cols] * ubuf[7:7 + tm, cols]
                + cfw_ref[2:3, cols] * u + cfb_ref[:, cols])

    for j in range(n_col):
        fa = conv(j * width)
        fb = conv(d_ff + j * width)
        act[:, j * width:(j + 1) * width] = (fa * _sigmoid(fa) * fb).astype(BF16)
    tail = ubuf[tm + 6:tm + 8, :]
    fn_ref[...] = tail
    ubuf[6:8, :] = tail
    out = _dot(act[...], w_down_ref[...])
    xo_ref[...] = x + _rms(out, gpost_ref[...])


def _prompt_ffn(x, lw):
    bsz, seq, d_model = x.shape
    tm = PROMPT_TILE
    d_ff = lw["w_down"].shape[0]
    kern = functools.partial(_prompt_ffn_kernel, tm=tm, d_ff=d_ff)
    tile = pl.BlockSpec((None, tm, d_model), lambda b, t: (b, t, 0))
    return pl.pallas_call(
        kern,
        grid=(bsz, seq // tm),
        in_specs=[tile] + [_vmem_spec()] * 6,
        out_specs=[tile, pl.BlockSpec((None, 2, 2 * d_ff), lambda b, t: (b, 0, 0))],
        out_shape=[jax.ShapeDtypeStruct((bsz, seq, d_model), F32),
                   jax.ShapeDtypeStruct((bsz, 2, 2 * d_ff), F32)],
        scratch_shapes=[pltpu.VMEM((tm + 8, 2 * d_ff), F32),
                        pltpu.VMEM((tm, d_ff), BF16)],
        compiler_params=pltpu.CompilerParams(
            dimension_semantics=("arbitrary", "arbitrary"), vmem_limit_bytes=VMEM_LIMIT_BYTES),
        name="prompt_ffn",
    )(x, lw["w_up"], lw["cfw"], lw["cfb"], lw["w_down"], lw["gpre_f"], lw["gpost_f"])


def _time_steps(ref, n_seq, steps, cols=slice(None)):
    return [ref[t * n_seq:(t + 1) * n_seq, cols] for t in range(steps)]


def _sample_pre_kernel(x_ref, st_ref, w_in_ref, caw_ref, wexp_ref, bexp_ref, gv_ref,
                       w_ba_ref, w_bc_ref, gpre_ref,
                       q_ref, kn_ref, vn_ref, mac_ref, g1_ref, ca_ref, cv_ref,
                       *, n_seq, steps, d_model, a_w, c_w):
    b_w = N_KV * Q_PER_KV * HEAD_DIM
    kv_w = N_KV * HEAD_DIM
    off_q = 3 * a_w
    off_c = off_q + b_w + 2 * kv_w
    off_g = off_c + 2 * c_w
    h = _rms(x_ref[...], gpre_ref[...]).astype(BF16)
    step = lambda a, t: a[t * n_seq:(t + 1) * n_seq, :]

    pa = _dot(h, w_in_ref[:, 0:off_q])
    z = pa[:, 2 * a_w:3 * a_w] * pa[:, 0:a_w]
    hist = _time_steps(st_ref, n_seq, 2) + [step(z, t) for t in range(steps)]
    za = jnp.concatenate(
        [caw_ref[0:1, :] * hist[t] + caw_ref[1:2, :] * hist[t + 1] + caw_ref[2:3, :] * hist[t + 2]
         for t in range(steps)], axis=0)
    ya = pa[:, a_w:2 * a_w] * za
    for i in range(2):
        ca_ref[i * n_seq:(i + 1) * n_seq, :] = hist[steps + i]

    pq = _dot(h, w_in_ref[:, off_q:off_c])
    q_ref[...] = pq[:, 0:b_w]
    kn_ref[...] = pq[:, b_w:b_w + kv_w]
    vn_ref[...] = pq[:, b_w + kv_w:b_w + 2 * kv_w]

    pc = _dot(h, w_in_ref[:, off_c:off_g])
    vn = _rms(pc[:, c_w:2 * c_w], gv_ref[...])
    cv_ref[...] = vn
    mixed = []
    for t in range(steps):
        acc = bexp_ref[t:t + 1, :] + wexp_ref[t * steps:t * steps + 1, :] * step(vn, 0)
        for s in range(1, t + 1):
            acc += wexp_ref[t * steps + s:t * steps + s + 1, :] * step(vn, s)
        mixed.append(acc)
    yc = pc[:, 0:c_w] * jnp.concatenate(mixed, axis=0)

    mac_ref[...] = (_gates(h, w_in_ref, off_g, d_model, 0) * _dot(ya.astype(BF16), w_ba_ref[...])
                    + _gates(h, w_in_ref, off_g, d_model, 2) * _dot(yc.astype(BF16), w_bc_ref[...]))
    g1_ref[...] = _gates(h, w_in_ref, off_g, d_model, 1)


def _sample_pre(x2d, st2d, lw, n_seq, steps):
    rows, d_model = x2d.shape
    a_w = lw["caw"].shape[1]
    c_w = lw["gv"].shape[1]
    b_w = N_KV * Q_PER_KV * HEAD_DIM
    kv_w = N_KV * HEAD_DIM
    kern = functools.partial(_sample_pre_kernel, n_seq=n_seq, steps=steps, d_model=d_model,
                             a_w=a_w, c_w=c_w)
    return pl.pallas_call(
        kern,
        in_specs=[_vmem_spec()] * 10,
        out_specs=[_vmem_spec()] * 7,
        out_shape=[jax.ShapeDtypeStruct((rows, b_w), F32),
                   jax.ShapeDtypeStruct((rows, kv_w), F32),
                   jax.ShapeDtypeStruct((rows, kv_w), F32),
                   jax.ShapeDtypeStruct((rows, d_model), F32),
                   jax.ShapeDtypeStruct((rows, d_model), F32),
                   jax.ShapeDtypeStruct((2 * n_seq, a_w), F32),
                   jax.ShapeDtypeStruct((rows, c_w), F32)],
        compiler_params=pltpu.CompilerParams(vmem_limit_bytes=VMEM_LIMIT_BYTES),
        name="sample_pre",
    )(x2d, st2d, lw["w_in"], lw["caw"], lw["wexp"], lw["bexp"], lw["gv"],
      lw["w_ba"], lw["w_bc"], lw["gpre"])


def _sample_attn_kernel(q_ref, kn_ref, vn_ref, ck_ref, cv_ref, sinks_ref,
                        yb_ref, cko_ref, cvo_ref, *, steps, grp, cache_rows):
    nq = steps * grp
    nc = grp * cache_rows
    b_w = N_KV * Q_PER_KV * HEAD_DIM
    kv_w = N_KV * HEAD_DIM
    q = (q_ref[...].reshape(nq, b_w) * (HEAD_DIM ** -0.5)).astype(BF16)
    kn = kn_ref[...].reshape(nq, kv_w)
    vn = vn_ref[...].reshape(nq, kv_w)

    rows = Q_PER_KV * nq
    r = _mod(lax.broadcasted_iota(jnp.int32, (rows, nc), 0), nq)
    c = lax.broadcasted_iota(jnp.int32, (rows, nc), 1)
    in_cache = ((_div(c, cache_rows) == _mod(r, grp))
                & (_div(r, grp) + cache_rows - _mod(c, cache_rows) < WINDOW))
    r = _mod(lax.broadcasted_iota(jnp.int32, (rows, nq), 0), nq)
    c = lax.broadcasted_iota(jnp.int32, (rows, nq), 1)
    in_new = (_mod(c, grp) == _mod(r, grp)) & (_div(c, grp) <= _div(r, grp))

    def head_cache(ref, hh):
        return jnp.concatenate(
            [ref[j * kv_w + hh * HEAD_DIM:j * kv_w + (hh + 1) * HEAD_DIM, :] for j in range(grp)],
            axis=1).astype(BF16)

    outs = []
    for hh in range(N_KV):
        cols = slice(hh * HEAD_DIM, (hh + 1) * HEAD_DIM)
        qs = _stack_heads(q, hh)
        s_c = jnp.where(in_cache, _dot(qs, head_cache(ck_ref, hh)), NEG)
        s_n = jnp.where(in_new, _dot_nt(qs, kn[:, cols].astype(BF16)), NEG)
        sk = _sink_column(sinks_ref, hh, nq)
        m = jnp.maximum(jnp.maximum(jnp.max(s_c, axis=-1, keepdims=True),
                                    jnp.max(s_n, axis=-1, keepdims=True)), sk)
        p_c = jnp.exp(s_c - m)
        p_n = jnp.exp(s_n - m)
        den = (jnp.sum(p_c, axis=-1, keepdims=True) + jnp.sum(p_n, axis=-1, keepdims=True)
               + jnp.exp(sk - m))
        o = (_dot_nt(p_c.astype(BF16), head_cache(cv_ref, hh))
             + _dot(p_n.astype(BF16), vn[:, cols].astype(BF16))) * (1.0 / den)
        outs += [o[g * nq:(g + 1) * nq, :] for g in range(Q_PER_KV)]
    yb_ref[...] = jnp.concatenate(outs, axis=1).reshape(steps, grp, b_w)

    keep = cache_rows - steps
    for new, old, out in ((kn, ck_ref, cko_ref), (vn, cv_ref, cvo_ref)):
        out[:, 0:keep] = old[:, steps:cache_rows]
        new_t = new.T
        for j in range(grp):
            for t in range(steps):
                out[j * kv_w:(j + 1) * kv_w, keep + t:keep + t + 1] = new_t[:, t * grp + j:t * grp + j + 1]


def _sample_attn(q, kn, vn, ck_fm, cv_fm, sinks, n_seq, steps):
    grp = SAMPLE_SEQ_GROUP
    kv_w = N_KV * HEAD_DIM
    b_w = Q_PER_KV * kv_w
    cache_rows = ck_fm.shape[1]
    kern = functools.partial(_sample_attn_kernel, steps=steps, grp=grp, cache_rows=cache_rows)
    tm_spec = lambda w: pl.BlockSpec((steps, grp, w), lambda i: (0, i, 0))
    cache_spec = pl.BlockSpec((grp * kv_w, cache_rows), lambda i: (i, 0))
    return pl.pallas_call(
        kern,
        grid=(n_seq // grp,),
        in_specs=[tm_spec(b_w), tm_spec(kv_w), tm_spec(kv_w), cache_spec, cache_spec, _smem_spec()],
        out_specs=[tm_spec(b_w), cache_spec, cache_spec],
        out_shape=[jax.ShapeDtypeStruct((steps, n_seq, b_w), F32),
                   jax.ShapeDtypeStruct(ck_fm.shape, F32),
                   jax.ShapeDtypeStruct(cv_fm.shape, F32)],
        compiler_params=pltpu.CompilerParams(
            dimension_semantics=("arbitrary",), vmem_limit_bytes=VMEM_LIMIT_BYTES),
        name="sample_attn",
    )(q.reshape(steps, n_seq, b_w), kn.reshape(steps, n_seq, kv_w), vn.reshape(steps, n_seq, kv_w),
      ck_fm, cv_fm, sinks)


def _sample_post_kernel(x_ref, mac_ref, g1_ref, yb_ref, w_bb_ref, w_o_ref, gpost_ref, xo_ref):
    merged = mac_ref[...] + g1_ref[...] * _dot(yb_ref[...].astype(BF16), w_bb_ref[...])
    out = _dot(merged.astype(BF16), w_o_ref[...])
    xo_ref[...] = x_ref[...] + _rms(out, gpost_ref[...])


def _sample_post(x2d, mac, g1, yb, lw):
    return pl.pallas_call(
        _sample_post_kernel,
        in_specs=[_vmem_spec()] * 7,
        out_specs=_vmem_spec(),
        out_shape=jax.ShapeDtypeStruct(x2d.shape, F32),
        compiler_params=pltpu.CompilerParams(vmem_limit_bytes=VMEM_LIMIT_BYTES),
        name="sample_post",
    )(x2d, mac, g1, yb, lw["w_bb"], lw["w_o"], lw["gpost"])


def _sample_ffn_kernel(x_ref, st_ref, w_up_ref, cfw_ref, cfb_ref, w_down_ref, gpre_ref, gpost_ref,
                       xo_ref, fn_ref, act, *, n_seq, steps, d_ff):
    x = x_ref[...]
    h = _rms(x, gpre_ref[...]).astype(BF16)
    width, n_col = _ffn_columns(d_ff)

    def conv(lo):
        cols = slice(lo, lo + width)
        u = _dot(h, w_up_ref[:, cols])
        hist = (_time_steps(st_ref, n_seq, 2, cols)
                + [u[t * n_seq:(t + 1) * n_seq, :] for t in range(steps)])
        for i in range(2):
            fn_ref[i * n_seq:(i + 1) * n_seq, cols] = hist[steps + i]
        return jnp.concatenate(
            [cfw_ref[0:1, cols] * hist[t] + cfw_ref[1:2, cols] * hist[t + 1]
             + cfw_ref[2:3, cols] * hist[t + 2] + cfb_ref[:, cols] for t in range(steps)], axis=0)

    for j in range(n_col):
        fa = conv(j * width)
        fb = conv(d_ff + j * width)
        act[:, j * width:(j + 1) * width] = (fa * _sigmoid(fa) * fb).astype(BF16)
    out = _dot(act[...], w_down_ref[...])
    xo_ref[...] = x + _rms(out, gpost_ref[...])


def _sample_ffn(x_tm, st2d, lw, n_seq, steps):
    rows, _ = x_tm.shape
    d_ff = lw["w_down"].shape[0]
    kern = functools.partial(_sample_ffn_kernel, n_seq=n_seq, steps=steps, d_ff=d_ff)
    return pl.pallas_call(
        kern,
        in_specs=[_vmem_spec()] * 8,
        out_specs=[_vmem_spec()] * 2,
        out_shape=[jax.ShapeDtypeStruct(x_tm.shape, F32),
                   jax.ShapeDtypeStruct(st2d.shape, F32)],
        scratch_shapes=[pltpu.VMEM((rows, d_ff), BF16)],
        compiler_params=pltpu.CompilerParams(vmem_limit_bytes=VMEM_LIMIT_BYTES),
        name="sample_ffn",
    )(x_tm, st2d, lw["w_up"], lw["cfw"], lw["cfb"], lw["w_down"], lw["gpre_f"], lw["gpost_f"])


def _layer_weights(l, steps, w_in, conv_a_w, attn_sinks, spatial_w, spatial_b, g_v, w_branch_a,
                   w_branch_b, w_branch_c, w_out, g_pre_mix, g_post_mix, g_pre_ffn, g_post_ffn,
                   w_up, conv_ffn_w, conv_ffn_b, w_down):
    gdim = g_v.shape[1] // C_GROUPS
    row = lambda a: a[l][None, :]
    mm = lambda w: w[l].astype(BF16)
    return dict(
        w_in=mm(w_in), caw=conv_a_w[l], sinks=attn_sinks[l].reshape(-1), ws=spatial_w[l],
        bs_full=jnp.repeat(spatial_b[l].T, gdim, axis=1),
        wexp=jnp.repeat(spatial_w[l][:, :steps, :steps].reshape(C_GROUPS, steps * steps).T, gdim, axis=1),
        bexp=jnp.repeat(spatial_b[l][:, :steps].T, gdim, axis=1),
        gv=row(g_v), w_ba=mm(w_branch_a), w_bb=mm(w_branch_b), w_bc=mm(w_branch_c), w_o=mm(w_out),
        gpre=row(g_pre_mix), gpost=row(g_post_mix), gpre_f=row(g_pre_ffn), gpost_f=row(g_post_ffn),
        w_up=mm(w_up), cfw=conv_ffn_w[l], cfb=row(conv_ffn_b), w_down=mm(w_down))


def kernel(x_prompt, x_sample, state_conv_a, cache_win_k, cache_win_v, state_ffn_conv, w_in, conv_a_w, attn_sinks, spatial_w, spatial_b, g_v, w_branch_a, w_branch_b, w_branch_c, w_out, g_pre_mix, g_post_mix, g_pre_ffn, g_post_ffn, w_up, conv_ffn_w, conv_ffn_b, w_down):
    depth = w_in.shape[0]
    n_seq, steps, d_model = x_sample.shape
    bsz = x_prompt.shape[0]
    assert x_prompt.shape[1] % PROMPT_TILE == 0 and PROMPT_TILE % WINDOW == 0
    assert steps <= CHUNK and n_seq % SAMPLE_SEQ_GROUP == 0 and cache_win_k.shape[2] == WINDOW

    time_major = lambda a: jnp.swapaxes(a, -3, -2)
    yp = x_prompt
    ys = time_major(x_sample).reshape(steps * n_seq, d_model)
    st_a = time_major(state_conv_a).reshape(depth, 2 * n_seq, -1)
    st_f = time_major(state_ffn_conv).reshape(depth, 2 * n_seq, -1)
    kv_w = N_KV * HEAD_DIM
    feature_major = lambda c: jnp.transpose(c, (0, 1, 3, 4, 2)).reshape(depth, n_seq * kv_w, WINDOW)
    window_out = lambda c: jnp.transpose(c.reshape(depth, -1, N_KV, HEAD_DIM, WINDOW), (0, 1, 4, 2, 3))
    ck_fm, cv_fm = feature_major(cache_win_k), feature_major(cache_win_v)
    outs = [[] for _ in range(9)]
    for l in range(depth):
        lw = _layer_weights(l, steps, w_in, conv_a_w, attn_sinks, spatial_w, spatial_b, g_v,
                            w_branch_a, w_branch_b, w_branch_c, w_out, g_pre_mix, g_post_mix,
                            g_pre_ffn, g_post_ffn, w_up, conv_ffn_w, conv_ffn_b, w_down)
        yp, ca_p, wk_p, wv_p = _prompt_mixer(yp, lw)
        yp, ff_p = _prompt_ffn(yp, lw)

        q, kn, vn, mac, g1, ca_s, cv_s = _sample_pre(ys, st_a[l], lw, n_seq, steps)
        yb, wk_s, wv_s = _sample_attn(q, kn, vn, ck_fm[l], cv_fm[l], lw["sinks"], n_seq, steps)
        ys = _sample_post(ys, mac, g1, yb.reshape(steps * n_seq, -1), lw)
        ys, ff_s = _sample_ffn(ys, st_f[l], lw, n_seq, steps)

        for dst, val in zip(outs, (
                ca_p, wk_p, wv_p, ff_p, ca_s.reshape(2, n_seq, -1), wk_s, wv_s,
                ff_s.reshape(2, n_seq, -1), cv_s.reshape(steps, n_seq, -1))):
            dst.append(val)
    outs = [jnp.stack(o) for o in outs]
    for i in (4, 7, 8):
        outs[i] = time_major(outs[i])
    for i in (1, 2, 5, 6):
        outs[i] = window_out(outs[i])
    return (yp, time_major(ys.reshape(steps, n_seq, d_model))) + tuple(outs)
```

```python
import functools

import jax
import jax.numpy as jnp
from jax import lax
from jax.experimental import pallas as pl
from jax.experimental.pallas import tpu as pltpu

F32 = jnp.float32
BF16 = jnp.bfloat16

HEAD_DIM = 64
N_KV = 2
Q_PER_KV = 4
KV_W = N_KV * HEAD_DIM
B_W = Q_PER_KV * KV_W
WINDOW = 128
CHUNK = 128
C_GROUPS = 4
EPS = 1e-6
NEG = -1e30

VMEM_LIMIT_BYTES = 56 * 1024 * 1024
PROMPT_TILE = 512
SAMPLE_SEQ_GROUP = 8
FFN_COLUMNS = 256


def _rms(x, g):
    return x * lax.rsqrt(jnp.mean(x * x, axis=-1, keepdims=True) + EPS) * g


def _sigmoid(x):
    return 1.0 / (1.0 + jnp.exp(-x))


def _dot(a, b):
    return jnp.dot(a, b, preferred_element_type=F32)


def _dot_nt(a, b):
    return lax.dot_general(a, b, (((1,), (1,)), ((), ())), preferred_element_type=F32)


def _log2(n):
    assert n > 0 and n & (n - 1) == 0, f"{n} must be a power of two"
    return n.bit_length() - 1


def _div(a, n):
    return lax.shift_right_logical(a, _log2(n))


def _mod(a, n):
    return a & ((1 << _log2(n)) - 1)


def _row(ref, layer, cols=slice(None)):
    return ref[layer:layer + 1, cols]


def _sink_column(sinks_ref, layer, kv_head, rows_per_head):
    n = Q_PER_KV * rows_per_head
    base = (layer * N_KV + kv_head) * Q_PER_KV
    grp = _div(lax.broadcasted_iota(jnp.int32, (n, 1), 0), rows_per_head)
    col = jnp.full((n, 1), sinks_ref[base + Q_PER_KV - 1], F32)
    for g in range(Q_PER_KV - 2, -1, -1):
        col = jnp.where(grp == g, sinks_ref[base + g], col)
    return col


def _stack_heads(q, kv_head):
    base = kv_head * Q_PER_KV
    return jnp.concatenate(
        [q[:, (base + g) * HEAD_DIM:(base + g + 1) * HEAD_DIM] for g in range(Q_PER_KV)], axis=0)


def _unstack_heads(o, rows):
    return [o[g * rows:(g + 1) * rows, :] for g in range(Q_PER_KV)]


def _gates(h, w_in_ref, gate_off, d_model, branch):
    lo = gate_off + branch * d_model
    return _sigmoid(_dot(h, w_in_ref[:, lo:lo + d_model]))


def _proj_offsets(a_w, c_w):
    off_q = 3 * a_w
    off_c = off_q + B_W + 2 * KV_W
    return off_q, off_c, off_c + 2 * c_w


def _vmem_spec():
    return pl.BlockSpec(memory_space=pltpu.VMEM)


def _smem_spec():
    return pl.BlockSpec(memory_space=pltpu.SMEM)


def _layer_spec(stacked, layer):
    zeros = (0,) * (stacked.ndim - 1)
    return pl.BlockSpec((None,) + stacked.shape[1:], lambda *_: (layer,) + zeros,
                        pipeline_mode=pl.Buffered(1))


def _params(semantics):
    return pltpu.CompilerParams(dimension_semantics=semantics, vmem_limit_bytes=VMEM_LIMIT_BYTES)


def _prompt_mixer_kernel(x_ref, w_in_ref, w_ba_ref, w_bb_ref, w_bc_ref, w_o_ref, sinks_ref,
                         caw_ref, ws_ref, bs_ref, gv_ref, gpre_ref, gpost_ref,
                         xo_ref, ca_ref, kn_ref, vn_ref,
                         zbuf, kbuf, vbuf, *, layer, tm):
    t = pl.program_id(1)
    d_model = x_ref.shape[-1]
    a_w = caw_ref.shape[-1]
    c_w = gv_ref.shape[-1]
    off_q, off_c, off_g = _proj_offsets(a_w, c_w)

    @pl.when(t == 0)
    def _():
        zbuf[0:8, :] = jnp.zeros((8, a_w), F32)
        kbuf[0:WINDOW, :] = jnp.zeros((WINDOW, KV_W), BF16)
        vbuf[0:WINDOW, :] = jnp.zeros((WINDOW, KV_W), BF16)

    x = x_ref[...]
    h = _rms(x, _row(gpre_ref, layer)).astype(BF16)

    pa = _dot(h, w_in_ref[:, 0:off_q])
    z = pa[:, 2 * a_w:3 * a_w] * pa[:, 0:a_w]
    zbuf[8:8 + tm, :] = z
    za = (caw_ref[layer, 0:1, :] * zbuf[6:6 + tm, :] + caw_ref[layer, 1:2, :] * zbuf[7:7 + tm, :]
          + caw_ref[layer, 2:3, :] * z)
    ya = pa[:, a_w:2 * a_w] * za
    tail = z[tm - 2:tm, :]
    ca_ref[...] = tail
    zbuf[6:8, :] = tail

    pq = _dot(h, w_in_ref[:, off_q:off_c])
    q = (pq[:, 0:B_W] * (HEAD_DIM ** -0.5)).astype(BF16)
    k = pq[:, B_W:B_W + KV_W]
    v = pq[:, B_W + KV_W:B_W + 2 * KV_W]
    kn_ref[...] = k[tm - WINDOW:tm, :].T
    vn_ref[...] = v[tm - WINDOW:tm, :].T
    kbuf[WINDOW:WINDOW + tm, :] = k.astype(BF16)
    vbuf[WINDOW:WINDOW + tm, :] = v.astype(BF16)

    rows = Q_PER_KV * WINDOW
    qi_pos = _mod(lax.broadcasted_iota(jnp.int32, (rows, 2 * WINDOW), 0), WINDOW)
    kj_pos = lax.broadcasted_iota(jnp.int32, (rows, 2 * WINDOW), 1)
    diff = qi_pos + WINDOW - kj_pos
    band = (diff >= 0) & (diff < WINDOW)
    band_first = band & (kj_pos + jnp.minimum(t, 1) * WINDOW >= WINDOW)
    sk = [_sink_column(sinks_ref, layer, hh, WINDOW) for hh in range(N_KV)]

    yb_blocks = []
    for i in range(tm // WINDOW):
        qi = q[i * WINDOW:(i + 1) * WINDOW, :]
        kk = kbuf[i * WINDOW:(i + 2) * WINDOW, :]
        vv = vbuf[i * WINDOW:(i + 2) * WINDOW, :]
        mask = band_first if i == 0 else band
        outs = []
        for hh in range(N_KV):
            cols = slice(hh * HEAD_DIM, (hh + 1) * HEAD_DIM)
            s = jnp.where(mask, _dot_nt(_stack_heads(qi, hh), kk[:, cols]), NEG)
            m = jnp.maximum(jnp.max(s, axis=-1, keepdims=True), sk[hh])
            p = jnp.exp(s - m)
            den = jnp.sum(p, axis=-1, keepdims=True) + jnp.exp(sk[hh] - m)
            outs += _unstack_heads(_dot(p.astype(BF16), vv[:, cols]) * (1.0 / den), WINDOW)
        yb_blocks.append(jnp.concatenate(outs, axis=1))
    yb = jnp.concatenate(yb_blocks, axis=0)
    kbuf[0:WINDOW, :] = kbuf[tm:tm + WINDOW, :]
    vbuf[0:WINDOW, :] = vbuf[tm:tm + WINDOW, :]

    pc = _dot(h, w_in_ref[:, off_c:off_g])
    c_u = pc[:, 0:c_w]
    vn = _rms(pc[:, c_w:2 * c_w], _row(gv_ref, layer))
    tri = (lax.broadcasted_iota(jnp.int32, (CHUNK, CHUNK), 0)
           >= lax.broadcasted_iota(jnp.int32, (CHUNK, CHUNK), 1))
    wt = [jnp.where(tri, ws_ref[layer, g], 0.0).astype(BF16) for g in range(C_GROUPS)]
    lane_grp = _div(lax.broadcasted_iota(jnp.int32, (CHUNK, c_w), 1), c_w // C_GROUPS)
    yc_blocks = []
    for i in range(tm // CHUNK):
        vnb = vn[i * CHUNK:(i + 1) * CHUNK, :].astype(BF16)
        mixed = _dot(wt[C_GROUPS - 1], vnb)
        for g in range(C_GROUPS - 2, -1, -1):
            mixed = jnp.where(lane_grp == g, _dot(wt[g], vnb), mixed)
        yc_blocks.append(c_u[i * CHUNK:(i + 1) * CHUNK, :] * (mixed + bs_ref[layer]))
    yc = jnp.concatenate(yc_blocks, axis=0)

    merged = _gates(h, w_in_ref, off_g, d_model, 0) * _dot(ya.astype(BF16), w_ba_ref[...])
    merged += _gates(h, w_in_ref, off_g, d_model, 1) * _dot(yb.astype(BF16), w_bb_ref[...])
    merged += _gates(h, w_in_ref, off_g, d_model, 2) * _dot(yc.astype(BF16), w_bc_ref[...])
    out = _dot(merged.astype(BF16), w_o_ref[...])
    xo_ref[...] = x + _rms(out, _row(gpost_ref, layer))


def _prompt_mixer(x, p, layer):
    bsz, seq, d_model = x.shape
    tm = PROMPT_TILE
    a_w = p["caw"].shape[-1]
    kern = functools.partial(_prompt_mixer_kernel, layer=layer, tm=tm)
    tile = pl.BlockSpec((None, tm, d_model), lambda b, t: (b, t, 0))
    per_seq = lambda r, c: pl.BlockSpec((None, r, c), lambda b, t: (b, 0, 0))
    weights = [p[n] for n in ("w_in", "w_ba", "w_bb", "w_bc", "w_o")]
    small = [p[n] for n in ("caw", "ws", "bs_full", "gv", "gpre", "gpost")]
    return pl.pallas_call(
        kern,
        grid=(bsz, seq // tm),
        in_specs=([tile] + [_layer_spec(w, layer) for w in weights] + [_smem_spec()]
                  + [_vmem_spec()] * len(small)),
        out_specs=[tile, per_seq(2, a_w), per_seq(KV_W, WINDOW), per_seq(KV_W, WINDOW)],
        out_shape=[jax.ShapeDtypeStruct((bsz, seq, d_model), F32),
                   jax.ShapeDtypeStruct((bsz, 2, a_w), F32),
                   jax.ShapeDtypeStruct((bsz, KV_W, WINDOW), F32),
                   jax.ShapeDtypeStruct((bsz, KV_W, WINDOW), F32)],
        scratch_shapes=[pltpu.VMEM((tm + 8, a_w), F32),
                        pltpu.VMEM((tm + WINDOW, KV_W), BF16),
                        pltpu.VMEM((tm + WINDOW, KV_W), BF16)],
        compiler_params=_params(("arbitrary", "arbitrary")),
        name="prompt_mixer",
    )(x, *weights, p["sinks"], *small)


def _prompt_ffn_kernel(x_ref, w_up_ref, w_down_ref, cfw_ref, cfb_ref, gpre_ref, gpost_ref,
                       xo_ref, fn_ref, ubuf, act, *, layer, tm):
    t = pl.program_id(1)
    d_ff = w_down_ref.shape[0]
    width = FFN_COLUMNS

    @pl.when(t == 0)
    def _():
        ubuf[0:8, :] = jnp.zeros((8, 2 * d_ff), F32)

    x = x_ref[...]
    h = _rms(x, _row(gpre_ref, layer)).astype(BF16)

    def conv(lo):
        cols = slice(lo, lo + width)
        u = _dot(h, w_up_ref[:, cols])
        ubuf[8:8 + tm, cols] = u
        return (cfw_ref[layer, 0:1, cols] * ubuf[6:6 + tm, cols]
                + cfw_ref[layer, 1:2, cols] * ubuf[7:7 + tm, cols]
                + cfw_ref[layer, 2:3, cols] * u + _row(cfb_ref, layer, cols))

    for j in range(d_ff // width):
        fa = conv(j * width)
        fb = conv(d_ff + j * width)
        act[:, j * width:(j + 1) * width] = (fa * _sigmoid(fa) * fb).astype(BF16)
    tail = ubuf[tm + 6:tm + 8, :]
    fn_ref[...] = tail
    ubuf[6:8, :] = tail
    out = _dot(act[...], w_down_ref[...])
    xo_ref[...] = x + _rms(out, _row(gpost_ref, layer))


def _prompt_ffn(x, p, layer):
    bsz, seq, d_model = x.shape
    tm = PROMPT_TILE
    d_ff = p["w_down"].shape[1]
    assert d_ff % FFN_COLUMNS == 0
    kern = functools.partial(_prompt_ffn_kernel, layer=layer, tm=tm)
    tile = pl.BlockSpec((None, tm, d_model), lambda b, t: (b, t, 0))
    weights = [p["w_up"], p["w_down"]]
    small = [p[n] for n in ("cfw", "cfb", "gpre_f", "gpost_f")]
    return pl.pallas_call(
        kern,
        grid=(bsz, seq // tm),
        in_specs=[tile] + [_layer_spec(w, layer) for w in weights] + [_vmem_spec()] * len(small),
        out_specs=[tile, pl.BlockSpec((None, 2, 2 * d_ff), lambda b, t: (b, 0, 0))],
        out_shape=[jax.ShapeDtypeStruct((bsz, seq, d_model), F32),
                   jax.ShapeDtypeStruct((bsz, 2, 2 * d_ff), F32)],
        scratch_shapes=[pltpu.VMEM((tm + 8, 2 * d_ff), F32),
                        pltpu.VMEM((tm, d_ff), BF16)],
        compiler_params=_params(("arbitrary", "arbitrary")),
        name="prompt_ffn",
    )(x, *weights, *small)


def _time_steps(a, n_seq, steps):
    return [a[t * n_seq:(t + 1) * n_seq, :] for t in range(steps)]


def _sample_pre_kernel(x_ref, st_ref, w_in_ref, w_ba_ref, w_bc_ref,
                       caw_ref, wexp_ref, bexp_ref, gv_ref, gpre_ref,
                       q_ref, kn_ref, vn_ref, mac_ref, g1_ref, ca_ref, cv_ref,
                       *, layer, n_seq, steps):
    d_model = x_ref.shape[-1]
    a_w = caw_ref.shape[-1]
    c_w = gv_ref.shape[-1]
    off_q, off_c, off_g = _proj_offsets(a_w, c_w)
    h = _rms(x_ref[...], _row(gpre_ref, layer)).astype(BF16)

    pa = _dot(h, w_in_ref[:, 0:off_q])
    z = pa[:, 2 * a_w:3 * a_w] * pa[:, 0:a_w]
    hist = _time_steps(st_ref[layer], n_seq, 2) + _time_steps(z, n_seq, steps)
    za = jnp.concatenate(
        [caw_ref[layer, 0:1, :] * hist[t] + caw_ref[layer, 1:2, :] * hist[t + 1]
         + caw_ref[layer, 2:3, :] * hist[t + 2] for t in range(steps)], axis=0)
    ya = pa[:, a_w:2 * a_w] * za
    for i in range(2):
        ca_ref[i * n_seq:(i + 1) * n_seq, :] = hist[steps + i]

    pq = _dot(h, w_in_ref[:, off_q:off_c])
    q_ref[...] = pq[:, 0:B_W]
    for t in range(steps):
        rows = slice(t * n_seq, (t + 1) * n_seq)
        kn_ref[pl.ds(t, n_seq, stride=steps), :] = pq[rows, B_W:B_W + KV_W]
        vn_ref[pl.ds(t, n_seq, stride=steps), :] = pq[rows, B_W + KV_W:B_W + 2 * KV_W]

    pc = _dot(h, w_in_ref[:, off_c:off_g])
    vn = _rms(pc[:, c_w:2 * c_w], _row(gv_ref, layer))
    cv_ref[...] = vn
    vn_t = _time_steps(vn, n_seq, steps)
    mixed = []
    for t in range(steps):
        acc = bexp_ref[layer, t:t + 1, :] + wexp_ref[layer, t * steps:t * steps + 1, :] * vn_t[0]
        for s in range(1, t + 1):
            acc += wexp_ref[layer, t * steps + s:t * steps + s + 1, :] * vn_t[s]
        mixed.append(acc)
    yc = pc[:, 0:c_w] * jnp.concatenate(mixed, axis=0)

    mac_ref[...] = (_gates(h, w_in_ref, off_g, d_model, 0) * _dot(ya.astype(BF16), w_ba_ref[...])
                    + _gates(h, w_in_ref, off_g, d_model, 2) * _dot(yc.astype(BF16), w_bc_ref[...]))
    g1_ref[...] = _gates(h, w_in_ref, off_g, d_model, 1)


def _sample_pre(x2d, st_a, p, layer, n_seq, steps):
    rows, d_model = x2d.shape
    a_w = p["caw"].shape[-1]
    c_w = p["gv"].shape[-1]
    kern = functools.partial(_sample_pre_kernel, layer=layer, n_seq=n_seq, steps=steps)
    weights = [p[n] for n in ("w_in", "w_ba", "w_bc")]
    small = [p[n] for n in ("caw", "wexp", "bexp", "gv", "gpre")]
    return pl.pallas_call(
        kern,
        grid=(1,),
        in_specs=[_vmem_spec()] * 2 + [_layer_spec(w, layer) for w in weights] + [_vmem_spec()] * len(small),
        out_specs=[_vmem_spec()] * 7,
        out_shape=[jax.ShapeDtypeStruct((rows, B_W), F32),
                   jax.ShapeDtypeStruct((rows, KV_W), F32),
                   jax.ShapeDtypeStruct((rows, KV_W), F32),
                   jax.ShapeDtypeStruct((rows, d_model), F32),
                   jax.ShapeDtypeStruct((rows, d_model), F32),
                   jax.ShapeDtypeStruct((2 * n_seq, a_w), F32),
                   jax.ShapeDtypeStruct((rows, c_w), F32)],
        compiler_params=_params(("arbitrary",)),
        name="sample_pre",
    )(x2d, st_a, *weights, *small)


def _sample_attn_kernel(q_ref, kn_ref, vn_ref, ck_ref, cv_ref, sinks_ref, *rest,
                        layer, steps, grp, cache_rows):
    yb_ref, cko_ref, cvo_ref = rest[-3:]
    nq = steps * grp
    nc = grp * cache_rows
    q = (q_ref[...].reshape(nq, B_W) * (HEAD_DIM ** -0.5)).astype(BF16)
    kn = kn_ref[...]
    vn = vn_ref[...]

    rows = Q_PER_KV * nq
    r = _mod(lax.broadcasted_iota(jnp.int32, (rows, nc), 0), nq)
    c = lax.broadcasted_iota(jnp.int32, (rows, nc), 1)
    in_cache = ((_div(c, cache_rows) == _mod(r, grp))
                & (_div(r, grp) + cache_rows - _mod(c, cache_rows) < WINDOW))
    r = _mod(lax.broadcasted_iota(jnp.int32, (rows, nq), 0), nq)
    c = lax.broadcasted_iota(jnp.int32, (rows, nq), 1)
    in_new = (_div(c, steps) == _mod(r, grp)) & (_mod(c, steps) <= _div(r, grp))

    def head_cache(ref, hh):
        return jnp.concatenate(
            [ref[j * KV_W + hh * HEAD_DIM:j * KV_W + (hh + 1) * HEAD_DIM, :] for j in range(grp)],
            axis=1).astype(BF16)

    outs = []
    for hh in range(N_KV):
        cols = slice(hh * HEAD_DIM, (hh + 1) * HEAD_DIM)
        qs = _stack_heads(q, hh)
        s_c = jnp.where(in_cache, _dot(qs, head_cache(ck_ref, hh)), NEG)
        s_n = jnp.where(in_new, _dot_nt(qs, kn[:, cols].astype(BF16)), NEG)
        sk = _sink_column(sinks_ref, layer, hh, nq)
        m = jnp.maximum(jnp.maximum(jnp.max(s_c, axis=-1, keepdims=True),
                                    jnp.max(s_n, axis=-1, keepdims=True)), sk)
        p_c = jnp.exp(s_c - m)
        p_n = jnp.exp(s_n - m)
        den = (jnp.sum(p_c, axis=-1, keepdims=True) + jnp.sum(p_n, axis=-1, keepdims=True)
               + jnp.exp(sk - m))
        o = (_dot_nt(p_c.astype(BF16), head_cache(cv_ref, hh))
             + _dot(p_n.astype(BF16), vn[:, cols].astype(BF16))) * (1.0 / den)
        outs += _unstack_heads(o, nq)
    yb_ref[...] = jnp.concatenate(outs, axis=1).reshape(steps, grp, B_W)

    keep = cache_rows - steps
    for new, old, out in ((kn, ck_ref, cko_ref), (vn, cv_ref, cvo_ref)):
        out[:, 0:keep] = old[:, steps:cache_rows]
        new_t = new.T
        for j in range(grp):
            out[j * KV_W:(j + 1) * KV_W, keep:cache_rows] = new_t[:, j * steps:(j + 1) * steps]


def _sample_attn(q, kn, vn, ck_fm, cv_fm, sinks, prev_windows, layer, n_seq, steps):
    grp = SAMPLE_SEQ_GROUP
    cache_rows = ck_fm.shape[-1]
    kern = functools.partial(_sample_attn_kernel, layer=layer, steps=steps, grp=grp,
                             cache_rows=cache_rows)
    tm_spec = pl.BlockSpec((steps, grp, B_W), lambda i: (0, i, 0))
    new_spec = pl.BlockSpec((grp * steps, KV_W), lambda i: (i, 0))
    cache_spec = pl.BlockSpec((None, grp * KV_W, cache_rows), lambda i: (layer, i, 0))
    any_spec = pl.BlockSpec(memory_space=pl.ANY)
    n_in = 6
    return pl.pallas_call(
        kern,
        grid=(n_seq // grp,),
        in_specs=([tm_spec, new_spec, new_spec, cache_spec, cache_spec, _smem_spec()]
                  + [any_spec] * len(prev_windows)),
        out_specs=[tm_spec, cache_spec, cache_spec],
        out_shape=[jax.ShapeDtypeStruct((steps, n_seq, B_W), F32),
                   jax.ShapeDtypeStruct(ck_fm.shape, F32),
                   jax.ShapeDtypeStruct(cv_fm.shape, F32)],
        input_output_aliases={n_in + i: 1 + i for i in range(len(prev_windows))},
        compiler_params=_params(("arbitrary",)),
        name="sample_attn",
    )(q.reshape(steps, n_seq, B_W), kn, vn, ck_fm, cv_fm, sinks, *prev_windows)


def _sample_post_kernel(x_ref, mac_ref, g1_ref, yb_ref, w_bb_ref, w_o_ref, gpost_ref, xo_ref,
                        *, layer):
    merged = mac_ref[...] + g1_ref[...] * _dot(yb_ref[...].astype(BF16), w_bb_ref[...])
    out = _dot(merged.astype(BF16), w_o_ref[...])
    xo_ref[...] = x_ref[...] + _rms(out, _row(gpost_ref, layer))


def _sample_post(x2d, mac, g1, yb, p, layer):
    weights = [p["w_bb"], p["w_o"]]
    return pl.pallas_call(
        functools.partial(_sample_post_kernel, layer=layer),
        grid=(1,),
        in_specs=[_vmem_spec()] * 4 + [_layer_spec(w, layer) for w in weights] + [_vmem_spec()],
        out_specs=_vmem_spec(),
        out_shape=jax.ShapeDtypeStruct(x2d.shape, F32),
        compiler_params=_params(("arbitrary",)),
        name="sample_post",
    )(x2d, mac, g1, yb, *weights, p["gpost"])


def _sample_ffn_kernel(x_ref, st_ref, w_up_ref, w_down_ref, cfw_ref, cfb_ref, gpre_ref, gpost_ref,
                       xo_ref, fn_ref, act, *, layer, n_seq, steps):
    d_ff = w_down_ref.shape[0]
    width = FFN_COLUMNS
    x = x_ref[...]
    h = _rms(x, _row(gpre_ref, layer)).astype(BF16)

    def conv(lo):
        cols = slice(lo, lo + width)
        u = _dot(h, w_up_ref[:, cols])
        hist = _time_steps(st_ref[:, cols], n_seq, 2) + _time_steps(u, n_seq, steps)
        for i in range(2):
            fn_ref[i * n_seq:(i + 1) * n_seq, cols] = hist[steps + i]
        return jnp.concatenate(
            [cfw_ref[layer, 0:1, cols] * hist[t] + cfw_ref[layer, 1:2, cols] * hist[t + 1]
             + cfw_ref[layer, 2:3, cols] * hist[t + 2] + _row(cfb_ref, layer, cols)
             for t in range(steps)], axis=0)

    for j in range(d_ff // width):
        fa = conv(j * width)
        fb = conv(d_ff + j * width)
        act[:, j * width:(j + 1) * width] = (fa * _sigmoid(fa) * fb).astype(BF16)
    out = _dot(act[...], w_down_ref[...])
    xo_ref[...] = x + _rms(out, _row(gpost_ref, layer))


def _sample_ffn(x_tm, st_f, p, layer, n_seq, steps):
    rows, _ = x_tm.shape
    d_ff = p["w_down"].shape[1]
    kern = functools.partial(_sample_ffn_kernel, layer=layer, n_seq=n_seq, steps=steps)
    weights = [p["w_up"], p["w_down"]]
    small = [p[n] for n in ("cfw", "cfb", "gpre_f", "gpost_f")]
    return pl.pallas_call(
        kern,
        grid=(1,),
        in_specs=([_vmem_spec(), _layer_spec(st_f, layer)] + [_layer_spec(w, layer) for w in weights]
                  + [_vmem_spec()] * len(small)),
        out_specs=[_vmem_spec()] * 2,
        out_shape=[jax.ShapeDtypeStruct(x_tm.shape, F32),
                   jax.ShapeDtypeStruct(st_f.shape[1:], F32)],
        scratch_shapes=[pltpu.VMEM((rows, d_ff), BF16)],
        compiler_params=_params(("arbitrary",)),
        name="sample_ffn",
    )(x_tm, st_f, *weights, *small)


def _stacked_params(steps, w_in, conv_a_w, attn_sinks, spatial_w, spatial_b, g_v, w_branch_a,
                    w_branch_b, w_branch_c, w_out, g_pre_mix, g_post_mix, g_pre_ffn, g_post_ffn,
                    w_up, conv_ffn_w, conv_ffn_b, w_down):
    depth = w_in.shape[0]
    gdim = g_v.shape[1] // C_GROUPS
    lanes = lambda a: jnp.repeat(jnp.swapaxes(a, 1, 2), gdim, axis=2)
    mm = lambda w: w.astype(BF16)
    return dict(
        w_in=mm(w_in), w_ba=mm(w_branch_a), w_bb=mm(w_branch_b), w_bc=mm(w_branch_c), w_o=mm(w_out),
        w_up=mm(w_up), w_down=mm(w_down),
        caw=conv_a_w, sinks=attn_sinks.reshape(-1), ws=spatial_w, bs_full=lanes(spatial_b),
        wexp=lanes(spatial_w[:, :, :steps, :steps].reshape(depth, C_GROUPS, steps * steps)),
        bexp=lanes(spatial_b[:, :, :steps]),
        gv=g_v, gpre=g_pre_mix, gpost=g_post_mix, gpre_f=g_pre_ffn, gpost_f=g_post_ffn,
        cfw=conv_ffn_w, cfb=conv_ffn_b)


def kernel(x_prompt, x_sample, state_conv_a, cache_win_k, cache_win_v, state_ffn_conv, w_in, conv_a_w, attn_sinks, spatial_w, spatial_b, g_v, w_branch_a, w_branch_b, w_branch_c, w_out, g_pre_mix, g_post_mix, g_pre_ffn, g_post_ffn, w_up, conv_ffn_w, conv_ffn_b, w_down):
    depth = w_in.shape[0]
    n_seq, steps, d_model = x_sample.shape
    assert x_prompt.shape[1] % PROMPT_TILE == 0 and PROMPT_TILE % WINDOW == 0
    assert steps <= CHUNK and n_seq % SAMPLE_SEQ_GROUP == 0 and cache_win_k.shape[2] == WINDOW
    assert attn_sinks.shape[1:] == (N_KV, Q_PER_KV) and cache_win_k.shape[3:] == (N_KV, HEAD_DIM)
    p = _stacked_params(steps, w_in, conv_a_w, attn_sinks, spatial_w, spatial_b, g_v, w_branch_a,
                        w_branch_b, w_branch_c, w_out, g_pre_mix, g_post_mix, g_pre_ffn,
                        g_post_ffn, w_up, conv_ffn_w, conv_ffn_b, w_down)

    time_major = lambda a: jnp.swapaxes(a, -3, -2)
    yp = x_prompt
    ys = time_major(x_sample).reshape(steps * n_seq, d_model)
    st_a = time_major(state_conv_a).reshape(depth, 2 * n_seq, -1)
    st_f = time_major(state_ffn_conv).reshape(depth, 2 * n_seq, -1)
    feature_major = lambda c: jnp.transpose(c, (0, 1, 3, 4, 2)).reshape(depth, n_seq * KV_W, WINDOW)
    window_out = lambda c: jnp.transpose(c.reshape(depth, -1, N_KV, HEAD_DIM, WINDOW), (0, 1, 4, 2, 3))
    ck_fm, cv_fm = feature_major(cache_win_k), feature_major(cache_win_v)

    outs = [[] for _ in range(7)]
    windows_s = ()
    for l in range(depth):
        yp, ca_p, wk_p, wv_p = _prompt_mixer(yp, p, l)
        yp, ff_p = _prompt_ffn(yp, p, l)

        q, kn, vn, mac, g1, ca_s, cv_s = _sample_pre(ys, st_a, p, l, n_seq, steps)
        yb, *windows_s = _sample_attn(q, kn, vn, ck_fm, cv_fm, p["sinks"], windows_s, l, n_seq, steps)
        ys = _sample_post(ys, mac, g1, yb.reshape(steps * n_seq, -1), p, l)
        ys, ff_s = _sample_ffn(ys, st_f, p, l, n_seq, steps)

        for dst, val in zip(outs, (ca_p, wk_p, wv_p, ff_p, ca_s.reshape(2, n_seq, -1),
                                   ff_s.reshape(2, n_seq, -1), cv_s.reshape(steps, n_seq, -1))):
            dst.append(val)
    ca_p, wk_p, wv_p, ff_p, ca_s, ff_s, cv_s = (jnp.stack(o) for o in outs)
    return (yp, time_major(ys.reshape(steps, n_seq, d_model)),
            ca_p, window_out(wk_p), window_out(wv_p), ff_p,
            time_major(ca_s), window_out(windows_s[0]), window_out(windows_s[1]),
            time_major(ff_s), time_major(cv_s))
```

```python
import functools

import jax
import jax.numpy as jnp
from jax import lax
from jax.experimental import pallas as pl
from jax.experimental.pallas import tpu as pltpu

F32 = jnp.float32
BF16 = jnp.bfloat16

HEAD_DIM = 64
N_KV = 2
Q_PER_KV = 4
KV_W = N_KV * HEAD_DIM
B_W = Q_PER_KV * KV_W
WINDOW = 128
CHUNK = 128
C_GROUPS = 4
EPS = 1e-6
NEG = -1e30

VMEM_LIMIT_BYTES = 56 * 1024 * 1024
PROMPT_TILE = 512
SAMPLE_SEQ_GROUP = 8
FFN_COLUMNS = 256


def _rms(x, g):
    return x * lax.rsqrt(jnp.mean(x * x, axis=-1, keepdims=True) + EPS) * g


def _sigmoid(x):
    return 1.0 / (1.0 + jnp.exp(-x))


def _dot(a, b):
    return jnp.dot(a, b, preferred_element_type=F32)


def _dot_nt(a, b):
    return lax.dot_general(a, b, (((1,), (1,)), ((), ())), preferred_element_type=F32)


def _log2(n):
    assert n > 0 and n & (n - 1) == 0, f"{n} must be a power of two"
    return n.bit_length() - 1


def _div(a, n):
    return lax.shift_right_logical(a, _log2(n))


def _mod(a, n):
    return a & ((1 << _log2(n)) - 1)


def _row(ref, layer, cols=slice(None)):
    return ref[layer:layer + 1, cols]


def _sink_column(sinks_ref, layer, kv_head, rows_per_head):
    n = Q_PER_KV * rows_per_head
    base = (layer * N_KV + kv_head) * Q_PER_KV
    grp = _div(lax.broadcasted_iota(jnp.int32, (n, 1), 0), rows_per_head)
    col = jnp.full((n, 1), sinks_ref[base + Q_PER_KV - 1], F32)
    for g in range(Q_PER_KV - 2, -1, -1):
        col = jnp.where(grp == g, sinks_ref[base + g], col)
    return col


def _stack_heads(q, kv_head):
    base = kv_head * Q_PER_KV
    return jnp.concatenate(
        [q[:, (base + g) * HEAD_DIM:(base + g + 1) * HEAD_DIM] for g in range(Q_PER_KV)], axis=0)


def _unstack_heads(o, rows):
    return [o[g * rows:(g + 1) * rows, :] for g in range(Q_PER_KV)]


def _gates(h, w_in_ref, gate_off, d_model, branch):
    lo = gate_off + branch * d_model
    return _sigmoid(_dot(h, w_in_ref[:, lo:lo + d_model]))


def _proj_offsets(a_w, c_w):
    off_q = 3 * a_w
    off_c = off_q + B_W + 2 * KV_W
    return off_q, off_c, off_c + 2 * c_w


def _vmem_spec():
    return pl.BlockSpec(memory_space=pltpu.VMEM)


def _smem_spec():
    return pl.BlockSpec(memory_space=pltpu.SMEM)


def _any_spec():
    return pl.BlockSpec(memory_space=pl.ANY)


def _layer_spec(stacked, layer):
    zeros = (0,) * (stacked.ndim - 1)
    return pl.BlockSpec((None,) + stacked.shape[1:], lambda *_: (layer,) + zeros,
                        pipeline_mode=pl.Buffered(1))


WEIGHT_CHUNK_ROWS = 128


def _load_weight_bf16(w_hbm, layer, dst, stage, sems):
    k, n = dst.shape
    rc = stage.shape[1]
    assert k % rc == 0 and n <= stage.shape[2]
    n_chunks = k // rc

    def chunk_copy(c, slot):
        return pltpu.make_async_copy(w_hbm.at[layer, pl.ds(c * rc, rc), :],
                                     stage.at[slot, :, pl.ds(0, n)], sems.at[slot])

    chunk_copy(0, 0).start()

    def body(c, carry):
        slot = lax.rem(c, 2)

        @pl.when(c + 1 < n_chunks)
        def _():
            chunk_copy(c + 1, 1 - slot).start()

        chunk_copy(c, slot).wait()
        dst[pl.ds(pl.multiple_of(c * rc, rc), rc), :] = stage[slot, :, 0:n].astype(BF16)
        return carry

    lax.fori_loop(0, n_chunks, body, 0)


def _weight_scratch(weights, n_emit):
    widest = max(w.shape[2] for w in weights)
    return ([pltpu.VMEM(w.shape[1:], BF16) for w in weights]
            + [pltpu.VMEM((2, WEIGHT_CHUNK_ROWS, widest), F32),
               pltpu.SemaphoreType.DMA((2,)), pltpu.SemaphoreType.DMA((n_emit,))])


def _emit_copy(w_vmem, w_out, sems, i):
    return pltpu.make_async_copy(w_vmem[i], w_out[i], sems.at[i])


def _params(semantics):
    return pltpu.CompilerParams(dimension_semantics=semantics, vmem_limit_bytes=VMEM_LIMIT_BYTES)


N_MIXER_WEIGHTS = 5


def _prompt_mixer_kernel(x_ref, *refs, layer, tm):
    nw = N_MIXER_WEIGHTS
    w_hbm, refs = refs[:nw], refs[nw:]
    sinks_ref, caw_ref, ws_ref, bs_ref, gv_ref, gpre_ref, gpost_ref = refs[:7]
    xo_ref, ca_ref, kn_ref, vn_ref = refs[7:11]
    w_out, refs = refs[11:11 + nw], refs[11 + nw:]
    zbuf, kbuf, vbuf = refs[:3]
    w_vmem, (stage, load_sems, emit_sems) = refs[3:3 + nw], refs[3 + nw:]
    w_in_ref, w_ba_ref, w_bb_ref, w_bc_ref, w_o_ref = w_vmem
    t = pl.program_id(1)
    first_step = jnp.logical_and(pl.program_id(0) == 0, t == 0)
    d_model = x_ref.shape[-1]
    a_w = caw_ref.shape[-1]
    c_w = gv_ref.shape[-1]
    off_q, off_c, off_g = _proj_offsets(a_w, c_w)

    @pl.when(first_step)
    def _():
        for src, dst in zip(w_hbm, w_vmem):
            _load_weight_bf16(src, layer, dst, stage, load_sems)
        for i in range(nw):
            _emit_copy(w_vmem, w_out, emit_sems, i).start()

    @pl.when(t == 0)
    def _():
        zbuf[0:8, :] = jnp.zeros((8, a_w), F32)
        kbuf[0:WINDOW, :] = jnp.zeros((WINDOW, KV_W), BF16)
        vbuf[0:WINDOW, :] = jnp.zeros((WINDOW, KV_W), BF16)

    x = x_ref[...]
    h = _rms(x, _row(gpre_ref, layer)).astype(BF16)

    pa = _dot(h, w_in_ref[:, 0:off_q])
    z = pa[:, 2 * a_w:3 * a_w] * pa[:, 0:a_w]
    zbuf[8:8 + tm, :] = z
    za = (caw_ref[layer, 0:1, :] * zbuf[6:6 + tm, :] + caw_ref[layer, 1:2, :] * zbuf[7:7 + tm, :]
          + caw_ref[layer, 2:3, :] * z)
    ya = pa[:, a_w:2 * a_w] * za
    tail = z[tm - 2:tm, :]
    ca_ref[...] = tail
    zbuf[6:8, :] = tail

    pq = _dot(h, w_in_ref[:, off_q:off_c])
    q = (pq[:, 0:B_W] * (HEAD_DIM ** -0.5)).astype(BF16)
    k = pq[:, B_W:B_W + KV_W]
    v = pq[:, B_W + KV_W:B_W + 2 * KV_W]
    kn_ref[...] = k[tm - WINDOW:tm, :].T
    vn_ref[...] = v[tm - WINDOW:tm, :].T
    kbuf[WINDOW:WINDOW + tm, :] = k.astype(BF16)
    vbuf[WINDOW:WINDOW + tm, :] = v.astype(BF16)

    rows = Q_PER_KV * WINDOW
    qi_pos = _mod(lax.broadcasted_iota(jnp.int32, (rows, 2 * WINDOW), 0), WINDOW)
    kj_pos = lax.broadcasted_iota(jnp.int32, (rows, 2 * WINDOW), 1)
    diff = qi_pos + WINDOW - kj_pos
    band = (diff >= 0) & (diff < WINDOW)
    band_first = band & (kj_pos + jnp.minimum(t, 1) * WINDOW >= WINDOW)
    sk = [_sink_column(sinks_ref, layer, hh, WINDOW) for hh in range(N_KV)]

    yb_blocks = []
    for i in range(tm // WINDOW):
        qi = q[i * WINDOW:(i + 1) * WINDOW, :]
        kk = kbuf[i * WINDOW:(i + 2) * WINDOW, :]
        vv = vbuf[i * WINDOW:(i + 2) * WINDOW, :]
        mask = band_first if i == 0 else band
        outs = []
        for hh in range(N_KV):
            cols = slice(hh * HEAD_DIM, (hh + 1) * HEAD_DIM)
            s = jnp.where(mask, _dot_nt(_stack_heads(qi, hh), kk[:, cols]), NEG)
            m = jnp.maximum(jnp.max(s, axis=-1, keepdims=True), sk[hh])
            p = jnp.exp(s - m)
            den = jnp.sum(p, axis=-1, keepdims=True) + jnp.exp(sk[hh] - m)
            outs += _unstack_heads(_dot(p.astype(BF16), vv[:, cols]) * (1.0 / den), WINDOW)
        yb_blocks.append(jnp.concatenate(outs, axis=1))
    yb = jnp.concatenate(yb_blocks, axis=0)
    kbuf[0:WINDOW, :] = kbuf[tm:tm + WINDOW, :]
    vbuf[0:WINDOW, :] = vbuf[tm:tm + WINDOW, :]

    pc = _dot(h, w_in_ref[:, off_c:off_g])
    c_u = pc[:, 0:c_w]
    vn = _rms(pc[:, c_w:2 * c_w], _row(gv_ref, layer))
    tri = (lax.broadcasted_iota(jnp.int32, (CHUNK, CHUNK), 0)
           >= lax.broadcasted_iota(jnp.int32, (CHUNK, CHUNK), 1))
    wt = [jnp.where(tri, ws_ref[layer, g], 0.0).astype(BF16) for g in range(C_GROUPS)]
    lane_grp = _div(lax.broadcasted_iota(jnp.int32, (CHUNK, c_w), 1), c_w // C_GROUPS)
    yc_blocks = []
    for i in range(tm // CHUNK):
        vnb = vn[i * CHUNK:(i + 1) * CHUNK, :].astype(BF16)
        mixed = _dot(wt[C_GROUPS - 1], vnb)
        for g in range(C_GROUPS - 2, -1, -1):
            mixed = jnp.where(lane_grp == g, _dot(wt[g], vnb), mixed)
        yc_blocks.append(c_u[i * CHUNK:(i + 1) * CHUNK, :] * (mixed + bs_ref[layer]))
    yc = jnp.concatenate(yc_blocks, axis=0)

    merged = _gates(h, w_in_ref, off_g, d_model, 0) * _dot(ya.astype(BF16), w_ba_ref[...])
    merged += _gates(h, w_in_ref, off_g, d_model, 1) * _dot(yb.astype(BF16), w_bb_ref[...])
    merged += _gates(h, w_in_ref, off_g, d_model, 2) * _dot(yc.astype(BF16), w_bc_ref[...])
    out = _dot(merged.astype(BF16), w_o_ref[...])
    xo_ref[...] = x + _rms(out, _row(gpost_ref, layer))

    @pl.when(first_step)
    def _():
        for i in range(nw):
            _emit_copy(w_vmem, w_out, emit_sems, i).wait()


def _prompt_mixer(x, p, layer):
    bsz, seq, d_model = x.shape
    tm = PROMPT_TILE
    a_w = p["caw"].shape[-1]
    kern = functools.partial(_prompt_mixer_kernel, layer=layer, tm=tm)
    tile = pl.BlockSpec((None, tm, d_model), lambda b, t: (b, t, 0))
    per_seq = lambda r, c: pl.BlockSpec((None, r, c), lambda b, t: (b, 0, 0))
    weights = [p[n] for n in ("w_in", "w_ba", "w_bb", "w_bc", "w_o")]
    assert len(weights) == N_MIXER_WEIGHTS
    small = [p[n] for n in ("caw", "ws", "bs_full", "gv", "gpre", "gpost")]
    res = pl.pallas_call(
        kern,
        grid=(bsz, seq // tm),
        in_specs=([tile] + [_any_spec()] * len(weights) + [_smem_spec()]
                  + [_vmem_spec()] * len(small)),
        out_specs=([tile, per_seq(2, a_w), per_seq(KV_W, WINDOW), per_seq(KV_W, WINDOW)]
                   + [_any_spec()] * len(weights)),
        out_shape=([jax.ShapeDtypeStruct((bsz, seq, d_model), F32),
                    jax.ShapeDtypeStruct((bsz, 2, a_w), F32),
                    jax.ShapeDtypeStruct((bsz, KV_W, WINDOW), F32),
                    jax.ShapeDtypeStruct((bsz, KV_W, WINDOW), F32)]
                   + [jax.ShapeDtypeStruct(w.shape[1:], BF16) for w in weights]),
        scratch_shapes=([pltpu.VMEM((tm + 8, a_w), F32),
                         pltpu.VMEM((tm + WINDOW, KV_W), BF16),
                         pltpu.VMEM((tm + WINDOW, KV_W), BF16)]
                        + _weight_scratch(weights, len(weights))),
        compiler_params=_params(("arbitrary", "arbitrary")),
        name="prompt_mixer",
    )(x, *weights, p["sinks"], *small)
    return res[0], res[1], res[2], res[3], res[4:]


N_FFN_WEIGHTS = 2


def _prompt_ffn_kernel(x_ref, *refs, layer, tm):
    nw = N_FFN_WEIGHTS
    w_hbm, refs = refs[:nw], refs[nw:]
    cfw_ref, cfb_ref, gpre_ref, gpost_ref, xo_ref, fn_ref = refs[:6]
    w_out, refs = refs[6:6 + nw], refs[6 + nw:]
    ubuf, act = refs[:2]
    w_vmem, (stage, load_sems, emit_sems) = refs[2:2 + nw], refs[2 + nw:]
    w_up_ref, w_down_ref = w_vmem
    t = pl.program_id(1)
    first_step = jnp.logical_and(pl.program_id(0) == 0, t == 0)
    d_ff = w_down_ref.shape[0]
    width = FFN_COLUMNS

    @pl.when(first_step)
    def _():
        for src, dst in zip(w_hbm, w_vmem):
            _load_weight_bf16(src, layer, dst, stage, load_sems)
        for i in range(nw):
            _emit_copy(w_vmem, w_out, emit_sems, i).start()

    @pl.when(t == 0)
    def _():
        ubuf[0:8, :] = jnp.zeros((8, 2 * d_ff), F32)

    x = x_ref[...]
    h = _rms(x, _row(gpre_ref, layer)).astype(BF16)

    def conv(lo):
        cols = slice(lo, lo + width)
        u = _dot(h, w_up_ref[:, cols])
        ubuf[8:8 + tm, cols] = u
        return (cfw_ref[layer, 0:1, cols] * ubuf[6:6 + tm, cols]
                + cfw_ref[layer, 1:2, cols] * ubuf[7:7 + tm, cols]
                + cfw_ref[layer, 2:3, cols] * u + _row(cfb_ref, layer, cols))

    for j in range(d_ff // width):
        fa = conv(j * width)
        fb = conv(d_ff + j * width)
        act[:, j * width:(j + 1) * width] = (fa * _sigmoid(fa) * fb).astype(BF16)
    tail = ubuf[tm + 6:tm + 8, :]
    fn_ref[...] = tail
    ubuf[6:8, :] = tail
    out = _dot(act[...], w_down_ref[...])
    xo_ref[...] = x + _rms(out, _row(gpost_ref, layer))

    @pl.when(first_step)
    def _():
        for i in range(nw):
            _emit_copy(w_vmem, w_out, emit_sems, i).wait()


def _prompt_ffn(x, p, layer):
    bsz, seq, d_model = x.shape
    tm = PROMPT_TILE
    d_ff = p["w_down"].shape[1]
    assert d_ff % FFN_COLUMNS == 0
    kern = functools.partial(_prompt_ffn_kernel, layer=layer, tm=tm)
    tile = pl.BlockSpec((None, tm, d_model), lambda b, t: (b, t, 0))
    weights = [p["w_up"], p["w_down"]]
    assert len(weights) == N_FFN_WEIGHTS
    small = [p[n] for n in ("cfw", "cfb", "gpre_f", "gpost_f")]
    res = pl.pallas_call(
        kern,
        grid=(bsz, seq // tm),
        in_specs=[tile] + [_any_spec()] * len(weights) + [_vmem_spec()] * len(small),
        out_specs=([tile, pl.BlockSpec((None, 2, 2 * d_ff), lambda b, t: (b, 0, 0))]
                   + [_any_spec()] * len(weights)),
        out_shape=([jax.ShapeDtypeStruct((bsz, seq, d_model), F32),
                    jax.ShapeDtypeStruct((bsz, 2, 2 * d_ff), F32)]
                   + [jax.ShapeDtypeStruct(w.shape[1:], BF16) for w in weights]),
        scratch_shapes=([pltpu.VMEM((tm + 8, 2 * d_ff), F32), pltpu.VMEM((tm, d_ff), BF16)]
                        + _weight_scratch(weights, len(weights))),
        compiler_params=_params(("arbitrary", "arbitrary")),
        name="prompt_ffn",
    )(x, *weights, *small)
    return res[0], res[1], res[2:]


def _time_steps(a, n_seq, steps):
    return [a[t * n_seq:(t + 1) * n_seq, :] for t in range(steps)]


def _sample_pre_kernel(x_ref, st_ref, w_in_ref, w_ba_ref, w_bc_ref,
                       caw_ref, wexp_ref, bexp_ref, gv_ref, gpre_ref,
                       q_ref, kn_ref, vn_ref, mac_ref, g1_ref, ca_ref, cv_ref,
                       *, layer, n_seq, steps):
    d_model = x_ref.shape[-1]
    a_w = caw_ref.shape[-1]
    c_w = gv_ref.shape[-1]
    off_q, off_c, off_g = _proj_offsets(a_w, c_w)
    h = _rms(x_ref[...], _row(gpre_ref, layer)).astype(BF16)

    pa = _dot(h, w_in_ref[:, 0:off_q])
    z = pa[:, 2 * a_w:3 * a_w] * pa[:, 0:a_w]
    hist = _time_steps(st_ref[layer], n_seq, 2) + _time_steps(z, n_seq, steps)
    za = jnp.concatenate(
        [caw_ref[layer, 0:1, :] * hist[t] + caw_ref[layer, 1:2, :] * hist[t + 1]
         + caw_ref[layer, 2:3, :] * hist[t + 2] for t in range(steps)], axis=0)
    ya = pa[:, a_w:2 * a_w] * za
    for i in range(2):
        ca_ref[i * n_seq:(i + 1) * n_seq, :] = hist[steps + i]

    pq = _dot(h, w_in_ref[:, off_q:off_c])
    q_ref[...] = pq[:, 0:B_W]
    for t in range(steps):
        rows = slice(t * n_seq, (t + 1) * n_seq)
        kn_ref[pl.ds(t, n_seq, stride=steps), :] = pq[rows, B_W:B_W + KV_W]
        vn_ref[pl.ds(t, n_seq, stride=steps), :] = pq[rows, B_W + KV_W:B_W + 2 * KV_W]

    pc = _dot(h, w_in_ref[:, off_c:off_g])
    vn = _rms(pc[:, c_w:2 * c_w], _row(gv_ref, layer))
    cv_ref[...] = vn
    vn_t = _time_steps(vn, n_seq, steps)
    mixed = []
    for t in range(steps):
        acc = bexp_ref[layer, t:t + 1, :] + wexp_ref[layer, t * steps:t * steps + 1, :] * vn_t[0]
        for s in range(1, t + 1):
            acc += wexp_ref[layer, t * steps + s:t * steps + s + 1, :] * vn_t[s]
        mixed.append(acc)
    yc = pc[:, 0:c_w] * jnp.concatenate(mixed, axis=0)

    mac_ref[...] = (_gates(h, w_in_ref, off_g, d_model, 0) * _dot(ya.astype(BF16), w_ba_ref[...])
                    + _gates(h, w_in_ref, off_g, d_model, 2) * _dot(yc.astype(BF16), w_bc_ref[...]))
    g1_ref[...] = _gates(h, w_in_ref, off_g, d_model, 1)


def _sample_pre(x2d, st_a, weights, p, layer, n_seq, steps):
    rows, d_model = x2d.shape
    a_w = p["caw"].shape[-1]
    c_w = p["gv"].shape[-1]
    kern = functools.partial(_sample_pre_kernel, layer=layer, n_seq=n_seq, steps=steps)
    small = [p[n] for n in ("caw", "wexp", "bexp", "gv", "gpre")]
    return pl.pallas_call(
        kern,
        in_specs=[_vmem_spec()] * (2 + len(weights) + len(small)),
        out_specs=[_vmem_spec()] * 7,
        out_shape=[jax.ShapeDtypeStruct((rows, B_W), F32),
                   jax.ShapeDtypeStruct((rows, KV_W), F32),
                   jax.ShapeDtypeStruct((rows, KV_W), F32),
                   jax.ShapeDtypeStruct((rows, d_model), F32),
                   jax.ShapeDtypeStruct((rows, d_model), F32),
                   jax.ShapeDtypeStruct((2 * n_seq, a_w), F32),
                   jax.ShapeDtypeStruct((rows, c_w), F32)],
        compiler_params=_params(None),
        name="sample_pre",
    )(x2d, st_a, *weights, *small)


def _sample_attn_kernel(q_ref, kn_ref, vn_ref, ck_ref, cv_ref, sinks_ref, *rest,
                        layer, steps, grp, cache_rows):
    yb_ref, cko_ref, cvo_ref = rest[-3:]
    if len(rest) == 3:
        for ref in (cko_ref, cvo_ref):
            for other in range(ref.shape[0]):
                if other != layer:
                    ref[other] = jnp.zeros(ref.shape[1:], F32)
        cko_ref, cvo_ref = cko_ref.at[layer], cvo_ref.at[layer]
    nq = steps * grp
    nc = grp * cache_rows
    q = (q_ref[...].reshape(nq, B_W) * (HEAD_DIM ** -0.5)).astype(BF16)
    kn = kn_ref[...]
    vn = vn_ref[...]

    rows = Q_PER_KV * nq
    r = _mod(lax.broadcasted_iota(jnp.int32, (rows, nc), 0), nq)
    c = lax.broadcasted_iota(jnp.int32, (rows, nc), 1)
    in_cache = ((_div(c, cache_rows) == _mod(r, grp))
                & (_div(r, grp) + cache_rows - _mod(c, cache_rows) < WINDOW))
    r = _mod(lax.broadcasted_iota(jnp.int32, (rows, nq), 0), nq)
    c = lax.broadcasted_iota(jnp.int32, (rows, nq), 1)
    in_new = (_div(c, steps) == _mod(r, grp)) & (_mod(c, steps) <= _div(r, grp))

    def head_cache(ref, hh):
        return jnp.concatenate(
            [ref[j * KV_W + hh * HEAD_DIM:j * KV_W + (hh + 1) * HEAD_DIM, :] for j in range(grp)],
            axis=1).astype(BF16)

    outs = []
    for hh in range(N_KV):
        cols = slice(hh * HEAD_DIM, (hh + 1) * HEAD_DIM)
        qs = _stack_heads(q, hh)
        s_c = jnp.where(in_cache, _dot(qs, head_cache(ck_ref, hh)), NEG)
        s_n = jnp.where(in_new, _dot_nt(qs, kn[:, cols].astype(BF16)), NEG)
        sk = _sink_column(sinks_ref, layer, hh, nq)
        m = jnp.maximum(jnp.maximum(jnp.max(s_c, axis=-1, keepdims=True),
                                    jnp.max(s_n, axis=-1, keepdims=True)), sk)
        p_c = jnp.exp(s_c - m)
        p_n = jnp.exp(s_n - m)
        den = (jnp.sum(p_c, axis=-1, keepdims=True) + jnp.sum(p_n, axis=-1, keepdims=True)
               + jnp.exp(sk - m))
        o = (_dot_nt(p_c.astype(BF16), head_cache(cv_ref, hh))
             + _dot(p_n.astype(BF16), vn[:, cols].astype(BF16))) * (1.0 / den)
        outs += _unstack_heads(o, nq)
    yb_ref[...] = jnp.concatenate(outs, axis=1).reshape(steps, grp, B_W)

    keep = cache_rows - steps
    for new, old, out in ((kn, ck_ref, cko_ref), (vn, cv_ref, cvo_ref)):
        out[:, 0:keep] = old[:, steps:cache_rows]
        new_t = new.T
        for j in range(grp):
            out[j * KV_W:(j + 1) * KV_W, keep:cache_rows] = new_t[:, j * steps:(j + 1) * steps]


def _sample_attn(q, kn, vn, ck_fm, cv_fm, sinks, prev_windows, layer, n_seq, steps):
    grp = SAMPLE_SEQ_GROUP
    cache_rows = ck_fm.shape[-1]
    kern = functools.partial(_sample_attn_kernel, layer=layer, steps=steps, grp=grp,
                             cache_rows=cache_rows)
    tm_spec = pl.BlockSpec((steps, grp, B_W), lambda i: (0, i, 0))
    new_spec = pl.BlockSpec((grp * steps, KV_W), lambda i: (i, 0))
    cache_spec = pl.BlockSpec((None, grp * KV_W, cache_rows), lambda i: (layer, i, 0))
    all_layers = pl.BlockSpec((ck_fm.shape[0], grp * KV_W, cache_rows), lambda i: (0, i, 0))
    window_spec = cache_spec if prev_windows else all_layers
    n_in = 6
    return pl.pallas_call(
        kern,
        grid=(n_seq // grp,),
        in_specs=([tm_spec, new_spec, new_spec, cache_spec, cache_spec, _smem_spec()]
                  + [_any_spec()] * len(prev_windows)),
        out_specs=[tm_spec, window_spec, window_spec],
        out_shape=[jax.ShapeDtypeStruct((steps, n_seq, B_W), F32),
                   jax.ShapeDtypeStruct(ck_fm.shape, F32),
                   jax.ShapeDtypeStruct(cv_fm.shape, F32)],
        input_output_aliases={n_in + i: 1 + i for i in range(len(prev_windows))},
        compiler_params=_params(("arbitrary",)),
        name="sample_attn",
    )(q.reshape(steps, n_seq, B_W), kn, vn, ck_fm, cv_fm, sinks, *prev_windows)


def _sample_post_kernel(x_ref, mac_ref, g1_ref, yb_ref, w_bb_ref, w_o_ref, gpost_ref, xo_ref,
                        *, layer):
    merged = mac_ref[...] + g1_ref[...] * _dot(yb_ref[...].astype(BF16), w_bb_ref[...])
    out = _dot(merged.astype(BF16), w_o_ref[...])
    xo_ref[...] = x_ref[...] + _rms(out, _row(gpost_ref, layer))


def _sample_post(x2d, mac, g1, yb, weights, p, layer):
    return pl.pallas_call(
        functools.partial(_sample_post_kernel, layer=layer),
        in_specs=[_vmem_spec()] * (5 + len(weights)),
        out_specs=_vmem_spec(),
        out_shape=jax.ShapeDtypeStruct(x2d.shape, F32),
        compiler_params=_params(None),
        name="sample_post",
    )(x2d, mac, g1, yb, *weights, p["gpost"])


def _sample_ffn_kernel(x_ref, st_ref, w_up_ref, w_down_ref, cfw_ref, cfb_ref, gpre_ref, gpost_ref,
                       xo_ref, fn_ref, act, *, layer, n_seq, steps):
    d_ff = w_down_ref.shape[0]
    width = FFN_COLUMNS
    x = x_ref[...]
    h = _rms(x, _row(gpre_ref, layer)).astype(BF16)

    def conv(lo):
        cols = slice(lo, lo + width)
        u = _dot(h, w_up_ref[:, cols])
        hist = _time_steps(st_ref[:, cols], n_seq, 2) + _time_steps(u, n_seq, steps)
        for i in range(2):
            fn_ref[i * n_seq:(i + 1) * n_seq, cols] = hist[steps + i]
        return jnp.concatenate(
            [cfw_ref[layer, 0:1, cols] * hist[t] + cfw_ref[layer, 1:2, cols] * hist[t + 1]
             + cfw_ref[layer, 2:3, cols] * hist[t + 2] + _row(cfb_ref, layer, cols)
             for t in range(steps)], axis=0)

    for j in range(d_ff // width):
        fa = conv(j * width)
        fb = conv(d_ff + j * width)
        act[:, j * width:(j + 1) * width] = (fa * _sigmoid(fa) * fb).astype(BF16)
    out = _dot(act[...], w_down_ref[...])
    xo_ref[...] = x + _rms(out, _row(gpost_ref, layer))


def _sample_ffn(x_tm, st_f, weights, p, layer, n_seq, steps):
    rows, _ = x_tm.shape
    d_ff = weights[1].shape[0]
    kern = functools.partial(_sample_ffn_kernel, layer=layer, n_seq=n_seq, steps=steps)
    small = [p[n] for n in ("cfw", "cfb", "gpre_f", "gpost_f")]
    return pl.pallas_call(
        kern,
        grid=(1,),
        in_specs=[_vmem_spec(), _layer_spec(st_f, layer)] + [_vmem_spec()] * (len(weights) + len(small)),
        out_specs=[_vmem_spec()] * 2,
        out_shape=[jax.ShapeDtypeStruct(x_tm.shape, F32),
                   jax.ShapeDtypeStruct(st_f.shape[1:], F32)],
        scratch_shapes=[pltpu.VMEM((rows, d_ff), BF16)],
        compiler_params=_params(("arbitrary",)),
        name="sample_ffn",
    )(x_tm, st_f, *weights, *small)


def _stacked_params(steps, w_in, conv_a_w, attn_sinks, spatial_w, spatial_b, g_v, w_branch_a,
                    w_branch_b, w_branch_c, w_out, g_pre_mix, g_post_mix, g_pre_ffn, g_post_ffn,
                    w_up, conv_ffn_w, conv_ffn_b, w_down):
    depth = w_in.shape[0]
    gdim = g_v.shape[1] // C_GROUPS
    lanes = lambda a: jnp.repeat(jnp.swapaxes(a, 1, 2), gdim, axis=2)
    return dict(
        w_in=w_in, w_ba=w_branch_a, w_bb=w_branch_b, w_bc=w_branch_c, w_o=w_out,
        w_up=w_up, w_down=w_down,
        caw=conv_a_w, sinks=attn_sinks.reshape(-1), ws=spatial_w, bs_full=lanes(spatial_b),
        wexp=lanes(spatial_w[:, :, :steps, :steps].reshape(depth, C_GROUPS, steps * steps)),
        bexp=lanes(spatial_b[:, :, :steps]),
        gv=g_v, gpre=g_pre_mix, gpost=g_post_mix, gpre_f=g_pre_ffn, gpost_f=g_post_ffn,
        cfw=conv_ffn_w, cfb=conv_ffn_b)


def kernel(x_prompt, x_sample, state_conv_a, cache_win_k, cache_win_v, state_ffn_conv, w_in, conv_a_w, attn_sinks, spatial_w, spatial_b, g_v, w_branch_a, w_branch_b, w_branch_c, w_out, g_pre_mix, g_post_mix, g_pre_ffn, g_post_ffn, w_up, conv_ffn_w, conv_ffn_b, w_down):
    depth = w_in.shape[0]
    n_seq, steps, d_model = x_sample.shape
    assert x_prompt.shape[1] % PROMPT_TILE == 0 and PROMPT_TILE % WINDOW == 0
    assert steps <= CHUNK and n_seq % SAMPLE_SEQ_GROUP == 0 and cache_win_k.shape[2] == WINDOW
    assert attn_sinks.shape[1:] == (N_KV, Q_PER_KV) and cache_win_k.shape[3:] == (N_KV, HEAD_DIM)
    p = _stacked_params(steps, w_in, conv_a_w, attn_sinks, spatial_w, spatial_b, g_v, w_branch_a,
                        w_branch_b, w_branch_c, w_out, g_pre_mix, g_post_mix, g_pre_ffn,
                        g_post_ffn, w_up, conv_ffn_w, conv_ffn_b, w_down)

    time_major = lambda a: jnp.swapaxes(a, -3, -2)
    yp = x_prompt
    ys = time_major(x_sample).reshape(steps * n_seq, d_model)
    st_a = time_major(state_conv_a).reshape(depth, 2 * n_seq, -1)
    st_f = time_major(state_ffn_conv).reshape(depth, 2 * n_seq, -1)
    feature_major = lambda c: jnp.transpose(c, (0, 1, 3, 4, 2)).reshape(depth, n_seq * KV_W, WINDOW)
    window_out = lambda c: jnp.transpose(c.reshape(depth, -1, N_KV, HEAD_DIM, WINDOW), (0, 1, 4, 2, 3))
    ck_fm, cv_fm = feature_major(cache_win_k), feature_major(cache_win_v)

    outs = [[] for _ in range(7)]
    windows_s = ()
    for l in range(depth):
        yp, ca_p, wk_p, wv_p, (w_in_l, w_ba_l, w_bb_l, w_bc_l, w_o_l) = _prompt_mixer(yp, p, l)
        yp, ff_p, ffn_w_l = _prompt_ffn(yp, p, l)

        q, kn, vn, mac, g1, ca_s, cv_s = _sample_pre(
            ys, st_a, (w_in_l, w_ba_l, w_bc_l), p, l, n_seq, steps)
        yb, *windows_s = _sample_attn(q, kn, vn, ck_fm, cv_fm, p["sinks"], windows_s, l, n_seq, steps)
        ys = _sample_post(ys, mac, g1, yb.reshape(steps * n_seq, -1), (w_bb_l, w_o_l), p, l)
        ys, ff_s = _sample_ffn(ys, st_f, ffn_w_l, p, l, n_seq, steps)

        for dst, val in zip(outs, (ca_p, wk_p, wv_p, ff_p, ca_s.reshape(2, n_seq, -1),
                                   ff_s.reshape(2, n_seq, -1), cv_s.reshape(steps, n_seq, -1))):
            dst.append(val)
    ca_p, wk_p, wv_p, ff_p, ca_s, ff_s, cv_s = (jnp.stack(o) for o in outs)
    return (yp, time_major(ys.reshape(steps, n_seq, d_model)),
            ca_p, window_out(wk_p), window_out(wv_p), ff_p,
            time_major(ca_s), window_out(windows_s[0]), window_out(windows_s[1]),
            time_major(ff_s), time_major(cv_s))
```
